```python
import math
import jax, jax.numpy as jnp
from jax import lax
import numpy as np

D_MODEL = 2048
BATCH = 2
SEQ = 16384
DEPTH = 4

DIFF_HEADS = 6
DIFF_HD = 64
DIFF_VD = 2 * DIFF_HD
GLA_HEADS = 4
GLA_DK = 64
GLA_DV = 128
GLA_GATE_RANK = 16
GLA_TAU = 16.0
GLA_CHUNK = 64
MLA_HEADS = 6
MLA_Q_RANK = 512
MLA_KV_RANK = 256
MLA_NOPE = 128
MLA_ROPE = 64
MLA_VD = 128
ROPE_THETA = 10000.0
D_FF = 5632
Q_BLOCK = 128
EPS = 1e-6

IN_SPLIT_SIZES = (
    DIFF_HEADS * 2 * DIFF_HD,
    DIFF_HEADS * 2 * DIFF_HD,
    DIFF_HEADS * DIFF_VD,
    GLA_HEADS * GLA_DK,
    GLA_HEADS * GLA_DK,
    GLA_HEADS * GLA_DV,
    GLA_GATE_RANK,
    GLA_HEADS * GLA_DV,
    MLA_Q_RANK,
    MLA_KV_RANK,
    MLA_ROPE,
)
IN_COLS = sum(IN_SPLIT_SIZES)
MIX_WIDTH = DIFF_HEADS * DIFF_VD + GLA_HEADS * GLA_DV + MLA_HEADS * MLA_VD

kernel_name = "hybrid_diffattn_gla_mla_parallel_heads"


def _rmsnorm(x, w):
    xf = x.astype(jnp.float32)
    y = xf * lax.rsqrt(jnp.mean(xf * xf, axis=-1, keepdims=True) + EPS)
    return (y * w.astype(jnp.float32)).astype(x.dtype)


def _split_cols(t, sizes):
    out, off = [], 0
    for s in sizes:
        out.append(t[..., off:off + s])
        off += s
    return out


def _rope(t, positions):
    r = t.shape[-1]
    inv = 1.0 / (ROPE_THETA ** (jnp.arange(0, r, 2, dtype=jnp.float32) / r))
    ang = positions.astype(jnp.float32)[:, :, None, None] * inv
    cos, sin = jnp.cos(ang), jnp.sin(ang)
    tf = t.astype(jnp.float32)
    t1, t2 = tf[..., : r // 2], tf[..., r // 2:]
    return jnp.concatenate([t1 * cos - t2 * sin, t1 * sin + t2 * cos], axis=-1).astype(t.dtype)


def _to_blocks(t):
    b, h, s, d = t.shape
    return t.reshape(b, h, s // Q_BLOCK, Q_BLOCK, d).transpose(2, 0, 1, 3, 4)


def _from_blocks(o):
    nb, b, h, q, d = o.shape
    return o.transpose(1, 2, 0, 3, 4).reshape(b, h, nb * q, d)


def _causal_probs(q_blk, k, blk, scale):
    s = jnp.einsum('bhqd,bhkd->bhqk', q_blk.astype(jnp.float32), k.astype(jnp.float32)) * scale
    q_pos = blk * Q_BLOCK + jnp.arange(q_blk.shape[2])
    k_pos = jnp.arange(k.shape[2])
    s = jnp.where(k_pos[None, :] <= q_pos[:, None], s, -jnp.inf)
    return jax.nn.softmax(s, axis=-1)


def _diff_attention(q1, q2, k1, k2, v, lam):
    scale = DIFF_HD ** -0.5
    nb = q1.shape[2] // Q_BLOCK

    def block(args):
        idx, q1b, q2b = args
        p = _causal_probs(q1b, k1, idx, scale) - lam * _causal_probs(q2b, k2, idx, scale)
        return jnp.einsum('bhqk,bhkd->bhqd', p.astype(v.dtype), v)

    o = lax.map(block, (jnp.arange(nb), _to_blocks(q1), _to_blocks(q2)))
    return _from_blocks(o)


def _causal_attention(q, k, v, scale):
    nb = q.shape[2] // Q_BLOCK

    def block(args):
        idx, qb = args
        p = _causal_probs(qb, k, idx, scale)
        return jnp.einsum('bhqk,bhkd->bhqd', p.astype(v.dtype), v)

    o = lax.map(block, (jnp.arange(nb), _to_blocks(q)))
    return _from_blocks(o)


def _gla_chunked(q, k, v, log_a):
    b, h, s, dk = q.shape
    dv = v.shape[-1]
    c = GLA_CHUNK
    nc = s // c

    def chunks(t):
        return t.astype(jnp.float32).reshape(b, h, nc, c, t.shape[-1]).transpose(2, 0, 1, 3, 4)

    qc, kc, vc = chunks(q), chunks(k), chunks(v)
    bc = jnp.cumsum(chunks(log_a), axis=3)
    causal = jnp.tril(jnp.ones((c, c), dtype=bool))

    def step(state, inp):
        qi, ki, vi, bi = inp
        inter = jnp.einsum('bhcd,bhde->bhce', qi * jnp.exp(bi), state)
        rel = bi[:, :, :, None, :] - bi[:, :, None, :, :]
        decay = jnp.exp(jnp.where(causal[:, :, None], rel, -jnp.inf))
        attn = jnp.einsum('bhid,bhjd,bhijd->bhij', qi, ki, decay)
        intra = jnp.einsum('bhij,bhjd->bhid', attn, vi)
        b_end = bi[:, :, -1:, :]
        state = (jnp.exp(b_end[:, :, 0, :])[..., None] * state
                 + jnp.einsum('bhcd,bhce->bhde', ki * jnp.exp(b_end - bi), vi))
        return state, inter + intra

    state0 = jnp.zeros((b, h, dk, dv), jnp.float32)
    _, out = lax.scan(step, state0, (qc, kc, vc, bc))
    return out.transpose(1, 2, 0, 3, 4).reshape(b, h, s, dv).astype(v.dtype)


def setup_inputs(seed: int = 0) -> dict:
    key = jax.random.key(seed)
    ks = jax.random.split(key, 24)
    f32 = jnp.float32

    def w(k, shape, fan_in):
        return jax.random.normal(k, shape, f32) * (fan_in ** -0.5)

    def gain(k, shape):
        return 1.0 + 0.02 * jax.random.normal(k, shape, f32)

    return {
        "x": jax.random.normal(ks[0], (BATCH, SEQ, D_MODEL), f32),
        "positions": jnp.broadcast_to(jnp.arange(SEQ, dtype=jnp.int32), (BATCH, SEQ)),
        "attn_norm": gain(ks[1], (DEPTH, D_MODEL)),
        "w_in": w(ks[2], (DEPTH, D_MODEL, IN_COLS), D_MODEL),
        "diff_lambda_q1": 0.1 * jax.random.normal(ks[3], (DEPTH, DIFF_HD), f32),
        "diff_lambda_k1": 0.1 * jax.random.normal(ks[4], (DEPTH, DIFF_HD), f32),
        "diff_lambda_q2": 0.1 * jax.random.normal(ks[5], (DEPTH, DIFF_HD), f32),
        "diff_lambda_k2": 0.1 * jax.random.normal(ks[6], (DEPTH, DIFF_HD), f32),
        "diff_subln": gain(ks[7], (DEPTH, DIFF_VD)),
        "gla_gate_w": w(ks[8], (DEPTH, GLA_GATE_RANK, GLA_HEADS * GLA_DK), GLA_GATE_RANK),
        "gla_gate_b": 0.01 * jax.random.normal(ks[9], (DEPTH, GLA_HEADS * GLA_DK), f32),
        "gla_norm": gain(ks[10], (DEPTH, GLA_DV)),
        "mla_q_norm": gain(ks[11], (DEPTH, MLA_Q_RANK)),
        "mla_w_q_b": w(ks[12], (DEPTH, MLA_Q_RANK, MLA_HEADS * (MLA_NOPE + MLA_ROPE)), MLA_Q_RANK),
        "mla_kv_norm": gain(ks[13], (DEPTH, MLA_KV_RANK)),
        "mla_w_kv_b": w(ks[14], (DEPTH, MLA_KV_RANK, MLA_HEADS * (MLA_NOPE + MLA_VD)), MLA_KV_RANK),
        "w_out": w(ks[15], (DEPTH, MIX_WIDTH, D_MODEL), MIX_WIDTH),
        "ffn_norm": gain(ks[16], (DEPTH, D_MODEL)),
        "w_gate": w(ks[17], (DEPTH, D_MODEL, D_FF), D_MODEL),
        "w_up": w(ks[18], (DEPTH, D_MODEL, D_FF), D_MODEL),
        "w_down": w(ks[19], (DEPTH, D_FF, D_MODEL), D_FF),
        "final_norm": gain(ks[20], (D_MODEL,)),
    }


def reference(x, positions, attn_norm, w_in, diff_lambda_q1, diff_lambda_k1, diff_lambda_q2,
              diff_lambda_k2, diff_subln, gla_gate_w, gla_gate_b, gla_norm, mla_q_norm, mla_w_q_b,
              mla_kv_norm, mla_w_kv_b, w_out, ffn_norm, w_gate, w_up, w_down, final_norm):
    f32 = jnp.float32
    b, s, _ = x.shape
    for l in range(DEPTH):
        h = _rmsnorm(x, attn_norm[l])
        proj = jnp.einsum('bsd,de->bse', h, w_in[l])
        (d_q, d_k, d_v, g_q, g_k, g_v, g_r, g_o, m_qa, m_kva, m_kr) = _split_cols(proj, IN_SPLIT_SIZES)

        dq = d_q.reshape(b, s, DIFF_HEADS, 2, DIFF_HD).transpose(0, 2, 3, 1, 4)
        dk = d_k.reshape(b, s, DIFF_HEADS, 2, DIFF_HD).transpose(0, 2, 3, 1, 4)
        dv = d_v.reshape(b, s, DIFF_HEADS, DIFF_VD).transpose(0, 2, 1, 3)
        lam_init = 0.8 - 0.6 * math.exp(-0.3 * l)
        lam = (jnp.exp(jnp.sum(diff_lambda_q1[l].astype(f32) * diff_lambda_k1[l].astype(f32)))
               - jnp.exp(jnp.sum(diff_lambda_q2[l].astype(f32) * diff_lambda_k2[l].astype(f32)))
               + lam_init)
        o_diff = _diff_attention(dq[:, :, 0], dq[:, :, 1], dk[:, :, 0], dk[:, :, 1], dv, lam)
        o_diff = _rmsnorm(o_diff, diff_subln[l]) * (1.0 - lam_init)
        o_diff = o_diff.transpose(0, 2, 1, 3).reshape(b, s, DIFF_HEADS * DIFF_VD)

        gq = g_q.reshape(b, s, GLA_HEADS, GLA_DK).transpose(0, 2, 1, 3) * (GLA_DK ** -0.5)
        gk = g_k.reshape(b, s, GLA_HEADS, GLA_DK).transpose(0, 2, 1, 3)
        gv = g_v.reshape(b, s, GLA_HEADS, GLA_DV).transpose(0, 2, 1, 3)
        gate_logits = jnp.einsum('bsr,re->bse', g_r, gla_gate_w[l]) + gla_gate_b[l]
        log_a = jax.nn.log_sigmoid(gate_logits.astype(f32)) / GLA_TAU
        log_a = log_a.reshape(b, s, GLA_HEADS, GLA_DK).transpose(0, 2, 1, 3)
        o_gla = _rmsnorm(_gla_chunked(gq, gk, gv, log_a), gla_norm[l])
        o_gla = o_gla.transpose(0, 2, 1, 3).reshape(b, s, GLA_HEADS * GLA_DV) * jax.nn.silu(g_o)

        mq = jnp.einsum('bsr,re->bse', _rmsnorm(m_qa, mla_q_norm[l]), mla_w_q_b[l])
        mq = mq.reshape(b, s, MLA_HEADS, MLA_NOPE + MLA_ROPE)
        mq = jnp.concatenate([mq[..., :MLA_NOPE], _rope(mq[..., MLA_NOPE:], positions)], axis=-1)
        mkv = jnp.einsum('bsr,re->bse', _rmsnorm(m_kva, mla_kv_norm[l]), mla_w_kv_b[l])
        mkv = mkv.reshape(b, s, MLA_HEADS, MLA_NOPE + MLA_VD)
        k_rope = jnp.broadcast_to(_rope(m_kr[:, :, None, :], positions), (b, s, MLA_HEADS, MLA_ROPE))
        mk = jnp.concatenate([mkv[..., :MLA_NOPE], k_rope], axis=-1)
        mv = mkv[..., MLA_NOPE:]
        o_mla = _causal_attention(mq.transpose(0, 2, 1, 3), mk.transpose(0, 2, 1, 3),
                                  mv.transpose(0, 2, 1, 3), (MLA_NOPE + MLA_ROPE) ** -0.5)
        o_mla = o_mla.transpose(0, 2, 1, 3).reshape(b, s, MLA_HEADS * MLA_VD)

        mixed = jnp.concatenate([o_diff, o_gla, o_mla], axis=-1)
        x = x + jnp.einsum('bse,ed->bsd', mixed, w_out[l])

        h2 = _rmsnorm(x, ffn_norm[l])
        ff = jax.nn.silu(jnp.einsum('bsd,df->bsf', h2, w_gate[l])) * jnp.einsum('bsd,df->bsf', h2, w_up[l])
        x = x + jnp.einsum('bsf,fd->bsd', ff, w_down[l])
    return _rmsnorm(x, final_norm)
```

```python
import functools
import math

import jax
import jax.numpy as jnp
from jax import lax
from jax.experimental import pallas as pl
from jax.experimental.pallas import tpu as pltpu

F32 = jnp.float32
BF16 = jnp.bfloat16

D_MODEL = 2048
DIFF_HEADS = 6
DIFF_HD = 64
DIFF_VD = 128
GLA_HEADS = 4
GLA_DK = 64
GLA_DV = 128
GLA_GATE_RANK = 16
GLA_TAU = 16.0
MLA_HEADS = 6
MLA_Q_RANK = 512
MLA_KV_RANK = 256
MLA_NOPE = 128
MLA_ROPE = 64
MLA_VD = 128
ROPE_THETA = 10000.0
D_FF = 5632
EPS = 1e-6
LOG2E = 1.4426950408889634

LANES = 128
VMEM_LIMIT = 56 * 1024 * 1024

PROJ_COLS = 5120
COL_DQ, COL_DK, COL_DV = 0, 768, 1536
COL_GQ, COL_GK, COL_MKVA, COL_GV, COL_GO, COL_MQA, COL_MKR, COL_GR = (
    2304, 2560, 2816, 3072, 3584, 4096, 4608, 4672)
MLA_QK = 256


def _cparams(sem):
    return pltpu.CompilerParams(dimension_semantics=sem, vmem_limit_bytes=VMEM_LIMIT)


def _rms(x, g):
    return x * lax.rsqrt(jnp.mean(x * x, axis=-1, keepdims=True) + EPS) * g


def _norm_matmul_kernel(x_ref, g_ref, w_ref, o_ref, h_ref):
    @pl.when(pl.program_id(1) == 0)
    def _():
        h_ref[...] = _rms(x_ref[...].astype(F32), g_ref[...]).astype(BF16)

    o_ref[...] = jnp.dot(h_ref[...], w_ref[...], preferred_element_type=F32).astype(o_ref.dtype)


def _norm_matmul(x, g, w, layer, *, tm, tn):
    t, d = x.shape
    n = w.shape[-1]
    return pl.pallas_call(
        _norm_matmul_kernel,
        grid=(t // tm, n // tn),
        in_specs=[
            pl.BlockSpec((tm, d), lambda i, j: (i, 0)),
            pl.BlockSpec((None, 1, d), lambda i, j: (layer, 0, 0)),
            pl.BlockSpec((None, d, tn), lambda i, j: (layer, 0, j)),
        ],
        out_specs=pl.BlockSpec((tm, tn), lambda i, j: (i, j)),
        out_shape=jax.ShapeDtypeStruct((t, n), BF16),
        scratch_shapes=[pltpu.VMEM((tm, d), BF16)],
        compiler_params=_cparams(("parallel", "arbitrary")),
        name="in_proj",
    )(x, g, w)


def _rope_table_kernel(pos_ref, inv_ref, cos_ref, sa_ref, sb_ref):
    ang = pos_ref[...] * inv_ref[...]
    c = jnp.cos(ang)
    s = jnp.sin(ang)
    lane = lax.broadcasted_iota(jnp.int32, ang.shape, 1)
    cos_ref[...] = jnp.where(lane < MLA_ROPE, c, 0.0)
    sa_ref[...] = jnp.where(lane < MLA_ROPE // 2, -s, 0.0)
    sb_ref[...] = jnp.where(lane < MLA_ROPE // 2, 0.0, jnp.where(lane < MLA_ROPE, s, 0.0))


def _rope_tables(posb, inv, *, tm):
    t = posb.shape[0]
    spec = pl.BlockSpec((tm, LANES), lambda i: (i, 0))
    return pl.pallas_call(
        _rope_table_kernel,
        grid=(t // tm,),
        in_specs=[spec, pl.BlockSpec((1, LANES), lambda i: (0, 0))],
        out_specs=[spec, spec, spec],
        out_shape=[jax.ShapeDtypeStruct((t, LANES), F32)] * 3,
        compiler_params=_cparams(("parallel",)),
        name="rope_tables",
    )(posb, inv)


def _rope128(x, cos, sa, sb):
    return x * cos + pltpu.roll(x, 96, 1) * sa + pltpu.roll(x, 32, 1) * sb


def _mla_prep_kernel(qa_ref, kva_ref, kr_ref, gq_ref, gkv_ref, wq_ref, wkv_ref,
                     cos_ref, sa_ref, sb_ref, q_ref, k_ref, v_ref):
    cos, sa, sb = cos_ref[...], sa_ref[...], sb_ref[...]
    scale = (MLA_NOPE + MLA_ROPE) ** -0.5 * LOG2E

    hq = _rms(qa_ref[...].astype(F32), gq_ref[...]).astype(BF16)
    q = jnp.dot(hq, wq_ref[...], preferred_element_type=F32)
    for h in range(MLA_HEADS):
        lo = h * MLA_QK
        q_ref[:, lo:lo + LANES] = (q[:, lo:lo + LANES] * scale).astype(BF16)
        rp = _rope128(q[:, lo + LANES:lo + 2 * LANES], cos, sa, sb)
        q_ref[:, lo + LANES:lo + 2 * LANES] = (rp * scale).astype(BF16)

    hk = _rms(kva_ref[...].astype(F32), gkv_ref[...]).astype(BF16)
    kv = jnp.dot(hk, wkv_ref[...], preferred_element_type=F32)
    lane = lax.broadcasted_iota(jnp.int32, cos.shape, 1)
    kr = jnp.where(lane < MLA_ROPE, kr_ref[...].astype(F32), 0.0)
    krr = _rope128(kr, cos, sa, sb).astype(BF16)
    for h in range(MLA_HEADS):
        lo = h * MLA_QK
        k_ref[:, lo:lo + LANES] = kv[:, h * LANES:(h + 1) * LANES].astype(BF16)
        k_ref[:, lo + LANES:lo + 2 * LANES] = krr
    v_ref[...] = kv[:, MLA_HEADS * LANES:].astype(BF16)


def _mla_prep(proj, gq, gkv, wq, wkv, cos, sa, sb, layer, *, tm):
    t = proj.shape[0]
    nq = MLA_HEADS * MLA_QK
    nv = MLA_HEADS * MLA_VD
    tab = pl.BlockSpec((tm, LANES), lambda i: (i, 0))
    return pl.pallas_call(
        _mla_prep_kernel,
        grid=(t // tm,),
        in_specs=[
            pl.BlockSpec((tm, MLA_Q_RANK), lambda i: (i, COL_MQA // MLA_Q_RANK)),
            pl.BlockSpec((tm, MLA_KV_RANK), lambda i: (i, COL_MKVA // MLA_KV_RANK)),
            pl.BlockSpec((tm, LANES), lambda i: (i, COL_MKR // LANES)),
            pl.BlockSpec((None, 1, MLA_Q_RANK), lambda i: (layer, 0, 0)),
            pl.BlockSpec((None, 1, MLA_KV_RANK), lambda i: (layer, 0, 0)),
            pl.BlockSpec((None, MLA_Q_RANK, nq), lambda i: (layer, 0, 0)),
            pl.BlockSpec((None, MLA_KV_RANK, nq), lambda i: (layer, 0, 0)),
            tab, tab, tab,
        ],
        out_specs=[
            pl.BlockSpec((tm, nq), lambda i: (i, 0)),
            pl.BlockSpec((tm, nq), lambda i: (i, 0)),
            pl.BlockSpec((tm, nv), lambda i: (i, 0)),
        ],
        out_shape=[
            jax.ShapeDtypeStruct((t, nq), BF16),
            jax.ShapeDtypeStruct((t, nq), BF16),
            jax.ShapeDtypeStruct((t, nv), BF16),
        ],
        compiler_params=_cparams(("parallel",)),
        name="mla_prep",
    )(proj, proj, proj, gq, gkv, wq, wkv, cos, sa, sb)


def _attn_kernel(*refs, n_maps, blk, diff):
    if diff:
        q_ref, k_ref, v_ref, lam_ref, cst_ref, subln_ref, o_ref, qt_s, m_s, l_s, acc_s = refs
    else:
        q_ref, k_ref, v_ref, o_ref, qt_s, m_s, l_s, acc_s = refs
    qi = pl.program_id(2)
    nq = n_maps * blk

    if diff:
        qf = q_ref[...].astype(F32) * (DIFF_HD ** -0.5 * LOG2E)
        qt = qf.T
        row = lax.broadcasted_iota(jnp.int32, qt.shape, 0)
        qt_s[:, :blk] = jnp.where(row < DIFF_HD, qt, 0.0).astype(BF16)
        qt_s[:, blk:] = jnp.where(row < DIFF_HD, 0.0, qt).astype(BF16)
    else:
        qt_s[...] = q_ref[...].astype(F32).T.astype(BF16)

    m_s[...] = jnp.full(m_s.shape, -jnp.inf, F32)
    l_s[...] = jnp.zeros(l_s.shape, F32)
    acc_s[...] = jnp.zeros(acc_s.shape, F32)

    def step(j, masked):
        off = pl.multiple_of(j * blk, blk)
        kb = k_ref[pl.ds(off, blk), :]
        vb = v_ref[pl.ds(off, blk), :]
        st = jnp.dot(kb, qt_s[...], preferred_element_type=F32)
        if masked:
            key = lax.broadcasted_iota(jnp.int32, st.shape, 0)
            qry = lax.broadcasted_iota(jnp.int32, st.shape, 1)
            qry = jnp.bitwise_and(qry, blk - 1)
            st = jnp.where(key <= qry, st, -jnp.inf)
        m_old = m_s[...]
        m_new = jnp.maximum(m_old, jnp.max(st, axis=0, keepdims=True))
        alpha = jnp.exp2(m_old - m_new)
        p = jnp.exp2(st - m_new)
        l_s[...] = alpha * l_s[...] + jnp.sum(p, axis=0, keepdims=True)
        pv = lax.dot_general(vb, p.astype(BF16), (((0,), (0,)), ((), ())),
                             preferred_element_type=F32)
        acc_s[...] = alpha * acc_s[...] + pv
        m_s[...] = m_new

    def body(j, carry):
        step(j, False)
        return carry

    lax.fori_loop(0, qi, body, 0)
    step(qi, True)

    ot = acc_s[...] * (1.0 / l_s[...])
    if diff:
        lp = lam_ref[...]
        lam_init = cst_ref[:, 0:1]
        lam = (jnp.exp(jnp.sum(lp[0:1] * lp[1:2], axis=-1, keepdims=True))
               - jnp.exp(jnp.sum(lp[2:3] * lp[3:4], axis=-1, keepdims=True)) + lam_init)
        d = (ot[:, :blk] - lam * ot[:, blk:]).T
        o_ref[...] = (_rms(d, subln_ref[...]) * (1.0 - lam_init)).astype(o_ref.dtype)
    else:
        o_ref[...] = ot.T.astype(o_ref.dtype)


def _attention(q, k, v, extras, *, batch, seq, heads, qk_dim, q_col, k_col, v_col, blk, diff, name):
    n_maps = 2 if diff else 1
    nqb = seq // blk
    kernel = functools.partial(_attn_kernel, n_maps=n_maps, blk=blk, diff=diff)
    in_specs = [
        pl.BlockSpec((blk, qk_dim), lambda b, h, i: (b * nqb + i, q_col + h)),
        pl.BlockSpec((seq, qk_dim), lambda b, h, i: (b, k_col + h)),
        pl.BlockSpec((seq, LANES), lambda b, h, i: (b, v_col + h)),
    ]
    in_specs += [spec for _, spec in extras]
    return pl.pallas_call(
        kernel,
        grid=(batch, heads, nqb),
        in_specs=in_specs,
        out_specs=pl.BlockSpec((blk, LANES), lambda b, h, i: (b * nqb + i, h)),
        out_shape=jax.ShapeDtypeStruct((batch * seq, heads * LANES), BF16),
        scratch_shapes=[
            pltpu.VMEM((qk_dim, n_maps * blk), BF16),
            pltpu.VMEM((1, n_maps * blk), F32),
            pltpu.VMEM((1, n_maps * blk), F32),
            pltpu.VMEM((LANES, n_maps * blk), F32),
        ],
        compiler_params=_cparams(("parallel", "parallel", "arbitrary")),
        name=name,
    )(q, k, v, *[a for a, _ in extras])


def _gla_kernel(gq_ref, gk_ref, gv_ref, go_ref, gr_ref, gw_ref, gb_ref, gn_ref, o_ref, st_ref, *, chunk):
    c = chunk

    @pl.when(pl.program_id(1) == 0)
    def _():
        st_ref[...] = jnp.zeros(st_ref.shape, F32)

    logits = jnp.dot(gr_ref[...], gw_ref[...], preferred_element_type=F32) + gb_ref[...]
    la = (jnp.minimum(logits, 0.0) - jnp.log(1.0 + jnp.exp(-jnp.abs(logits)))) * (1.0 / GLA_TAU)

    row = lax.broadcasted_iota(jnp.int32, la.shape, 0)
    b = la
    s = 1
    while s < c:
        b = b + jnp.where(row >= s, pltpu.roll(b, s, 0), 0.0)
        s *= 2

    q = gq_ref[...].astype(F32) * (GLA_DK ** -0.5)
    k = gk_ref[...].astype(F32)
    v = gv_ref[...]
    lane = lax.broadcasted_iota(jnp.int32, (c, LANES), 1)
    first_head = lane < GLA_DK

    ri = lax.broadcasted_iota(jnp.int32, (c, c), 0)
    ci = lax.broadcasted_iota(jnp.int32, (c, c), 1)
    lvl = jnp.where(ri > ci, jnp.bitwise_xor(ri, ci), 0)

    def head_scores(qt, kt):
        out = []
        for p in range(GLA_HEADS // 2):
            qp = qt[:, p * LANES:(p + 1) * LANES]
            kp = kt[:, p * LANES:(p + 1) * LANES].astype(BF16)
            for hh in range(2):
                qh = jnp.where(first_head if hh == 0 else jnp.logical_not(first_head), qp, 0.0)
                out.append(lax.dot_general(qh.astype(BF16), kp, (((1,), (1,)), ((), ())),
                                           preferred_element_type=F32))
        return out

    attn = [jnp.where(ri == ci, m, 0.0) for m in head_scores(q, k)]
    ref_b = b
    hs = 1
    bit = 0
    while hs < c:
        if hs > 1:
            ref_b = jnp.where(jnp.bitwise_and(row, hs - 1) < hs // 2, ref_b,
                              pltpu.roll(ref_b, hs // 2, 0))
        qt = q * jnp.exp(jnp.minimum(b - ref_b, 0.0))
        nxt = pltpu.roll(ref_b, c - hs, 0)
        kt = k * jnp.exp(jnp.minimum(nxt - b, 0.0))
        mask = lax.shift_right_logical(lvl, bit) == 1
        attn = [jnp.where(mask, m, a) for m, a in zip(head_scores(qt, kt), attn)]
        hs *= 2
        bit += 1

    b_end = b[c - 1:c, :]
    qb = (q * jnp.exp(b)).astype(BF16)
    kd = (k * jnp.exp(b_end - b)).astype(BF16)
    er = lax.broadcasted_iota(jnp.int32, (2 * GLA_DV, LANES), 0)
    ec = lax.broadcasted_iota(jnp.int32, (2 * GLA_DV, LANES), 1)
    bd_mask = (er < GLA_DV) == (ec < GLA_DK)

    outs = []
    for p in range(GLA_HEADS // 2):
        st = st_ref[p]
        inter = lax.dot_general(qb[:, p * LANES:(p + 1) * LANES], st.astype(BF16),
                                (((1,), (1,)), ((), ())), preferred_element_type=F32)
        vp = v[:, p * 2 * GLA_DV:(p + 1) * 2 * GLA_DV]
        for hh in range(2):
            h = 2 * p + hh
            intra = jnp.dot(attn[h].astype(BF16), vp[:, hh * GLA_DV:(hh + 1) * GLA_DV],
                            preferred_element_type=F32)
            outs.append(inter[:, hh * GLA_DV:(hh + 1) * GLA_DV] + intra)
        upd = lax.dot_general(vp, kd[:, p * LANES:(p + 1) * LANES], (((0,), (0,)), ((), ())),
                              preferred_element_type=F32)
        st_ref[p] = st * jnp.exp(b_end[:, p * LANES:(p + 1) * LANES]) + jnp.where(bd_mask, upd, 0.0)

    gn = gn_ref[...]
    for h in range(GLA_HEADS):
        g = go_ref[:, h * GLA_DV:(h + 1) * GLA_DV].astype(F32)
        y = _rms(outs[h], gn) * (g / (1.0 + jnp.exp(-g)))
        o_ref[:, h * GLA_DV:(h + 1) * GLA_DV] = y.astype(o_ref.dtype)


def _gla(proj, gw, gb, gn, layer, *, batch, seq, chunk):
    nc = seq // chunk
    qk = GLA_HEADS * GLA_DK
    vd = GLA_HEADS * GLA_DV
    return pl.pallas_call(
        functools.partial(_gla_kernel, chunk=chunk),
        grid=(batch, nc),
        in_specs=[
            pl.BlockSpec((chunk, qk), lambda b, c: (b * nc + c, COL_GQ // qk)),
            pl.BlockSpec((chunk, qk), lambda b, c: (b * nc + c, COL_GK // qk)),
            pl.BlockSpec((chunk, vd), lambda b, c: (b * nc + c, COL_GV // vd)),
            pl.BlockSpec((chunk, vd), lambda b, c: (b * nc + c, COL_GO // vd)),
            pl.BlockSpec((chunk, LANES), lambda b, c: (b * nc + c, COL_MKR // LANES)),
            pl.BlockSpec((None, LANES, qk), lambda b, c: (layer, 0, 0)),
            pl.BlockSpec((None, 1, qk), lambda b, c: (layer, 0, 0)),
            pl.BlockSpec((None, 1, GLA_DV), lambda b, c: (layer, 0, 0)),
        ],
        out_specs=pl.BlockSpec((chunk, vd), lambda b, c: (b * nc + c, 0)),
        out_shape=jax.ShapeDtypeStruct((batch * seq, vd), BF16),
        scratch_shapes=[pltpu.VMEM((GLA_HEADS // 2, 2 * GLA_DV, LANES), F32)],
        compiler_params=_cparams(("parallel", "arbitrary")),
        name="gla",
    )(proj, proj, proj, proj, proj, gw, gb, gn)


def _out_proj_kernel(x_ref, a_ref, b_ref, c_ref, wa_ref, wb_ref, wc_ref, o_ref):
    acc = jnp.dot(a_ref[...], wa_ref[...], preferred_element_type=F32)
    acc += jnp.dot(b_ref[...], wb_ref[...], preferred_element_type=F32)
    acc += jnp.dot(c_ref[...], wc_ref[...], preferred_element_type=F32)
    o_ref[...] = x_ref[...] + acc


def _out_proj(x, oa, ob, oc, wa, wb, wc, layer, *, tm):
    t, d = x.shape

    def act(a):
        return pl.BlockSpec((tm, a.shape[1]), lambda i: (i, 0))

    def wgt(w):
        return pl.BlockSpec((None, w.shape[1], d), lambda i: (layer, 0, 0))

    return pl.pallas_call(
        _out_proj_kernel,
        grid=(t // tm,),
        in_specs=[act(x), act(oa), act(ob), act(oc), wgt(wa), wgt(wb), wgt(wc)],
        out_specs=pl.BlockSpec((tm, d), lambda i: (i, 0)),
        out_shape=jax.ShapeDtypeStruct((t, d), F32),
        compiler_params=_cparams(("parallel",)),
        name="out_proj",
    )(x, oa, ob, oc, wa, wb, wc)


def _ffn_kernel(x_ref, g_ref, wg_ref, wu_ref, wd_ref, fg_ref, o_ref, h_ref, *, final_norm):
    f = pl.program_id(1)

    @pl.when(f == 0)
    def _():
        x = x_ref[...]
        h_ref[...] = _rms(x, g_ref[...]).astype(BF16)
        o_ref[...] = x

    h = h_ref[...]
    gate = jnp.dot(h, wg_ref[...], preferred_element_type=F32)
    up = jnp.dot(h, wu_ref[...], preferred_element_type=F32)
    act = (gate / (1.0 + jnp.exp(-gate)) * up).astype(BF16)
    o_ref[...] += jnp.dot(act, wd_ref[...], preferred_element_type=F32)

    if final_norm:
        @pl.when(f == pl.num_programs(1) - 1)
        def _():
            o_ref[...] = _rms(o_ref[...], fg_ref[...])


def _ffn(x, g, wg, wu, wd, fg, layer, *, tm, tf, final_norm):
    t, d = x.shape
    ff = wg.shape[-1]
    return pl.pallas_call(
        functools.partial(_ffn_kernel, final_norm=final_norm),
        grid=(t // tm, ff // tf),
        in_specs=[
            pl.BlockSpec((tm, d), lambda i, f: (i, 0)),
            pl.BlockSpec((None, 1, d), lambda i, f: (layer, 0, 0)),
            pl.BlockSpec((None, d, tf), lambda i, f: (layer, 0, f)),
            pl.BlockSpec((None, d, tf), lambda i, f: (layer, 0, f)),
            pl.BlockSpec((None, tf, d), lambda i, f: (layer, f, 0)),
            pl.BlockSpec((1, d), lambda i, f: (0, 0)),
        ],
        out_specs=pl.BlockSpec((tm, d), lambda i, f: (i, 0)),
        out_shape=jax.ShapeDtypeStruct((t, d), F32),
        scratch_shapes=[pltpu.VMEM((tm, d), BF16)],
        compiler_params=_cparams(("parallel", "arbitrary")),
        name="ffn",
    )(x, g, wg, wu, wd, fg)


def _prep_w_in(w_in):
    sizes = (768, 768, 768, 256, 256, 512, 16, 512, 512, 256, 64)
    offs = [0]
    for s in sizes:
        offs.append(offs[-1] + s)
    d_q, d_k, d_v, g_q, g_k, g_v, g_r, g_o, m_qa, m_kva, m_kr = (
        w_in[..., offs[i]:offs[i + 1]] for i in range(len(sizes)))
    used = COL_GR + GLA_GATE_RANK
    pad = jnp.zeros(w_in.shape[:-1] + (PROJ_COLS - used,), w_in.dtype)
    out = jnp.concatenate([d_q, d_k, d_v, g_q, g_k, m_kva, g_v, g_o, m_qa, m_kr, g_r, pad], axis=-1)
    return out.astype(BF16)


def _prep_w_q_b(w):
    depth, r, _ = w.shape
    w = w.reshape(depth, r, MLA_HEADS, MLA_NOPE + MLA_ROPE)
    w = jnp.pad(w, ((0, 0), (0, 0), (0, 0), (0, MLA_QK - MLA_NOPE - MLA_ROPE)))
    return w.reshape(depth, r, MLA_HEADS * MLA_QK).astype(BF16)


def _prep_w_kv_b(w):
    depth, r, _ = w.shape
    w = w.reshape(depth, r, MLA_HEADS, MLA_NOPE + MLA_VD)
    k = w[..., :MLA_NOPE].reshape(depth, r, MLA_HEADS * MLA_NOPE)
    v = w[..., MLA_NOPE:].reshape(depth, r, MLA_HEADS * MLA_VD)
    return jnp.concatenate([k, v], axis=-1).astype(BF16)


def _prep_gate_w(w):
    depth = w.shape[0]
    lo = COL_GR - COL_MKR
    out = jnp.zeros((depth, LANES, w.shape[-1]), w.dtype)
    return out.at[:, lo:lo + GLA_GATE_RANK, :].set(w).astype(BF16)


def kernel(x, positions, attn_norm, w_in, diff_lambda_q1, diff_lambda_k1, diff_lambda_q2, diff_lambda_k2, diff_subln, gla_gate_w, gla_gate_b, gla_norm, mla_q_norm, mla_w_q_b, mla_kv_norm, mla_w_kv_b, w_out, ffn_norm, w_gate, w_up, w_down, final_norm):
    batch, seq, d = x.shape
    depth = w_in.shape[0]
    t = batch * seq
    assert d == D_MODEL and seq % 512 == 0

    w_in_p = _prep_w_in(w_in)
    wq_p = _prep_w_q_b(mla_w_q_b)
    wkv_p = _prep_w_kv_b(mla_w_kv_b)
    gw_p = _prep_gate_w(gla_gate_w)
    n_a, n_b = DIFF_HEADS * DIFF_VD, GLA_HEADS * GLA_DV
    wo_a = w_out[:, :n_a].astype(BF16)
    wo_b = w_out[:, n_a:n_a + n_b].astype(BF16)
    wo_c = w_out[:, n_a + n_b:].astype(BF16)
    wg_p, wu_p, wd_p = w_gate.astype(BF16), w_up.astype(BF16), w_down.astype(BF16)
    lam_p = jnp.stack([diff_lambda_q1, diff_lambda_k1, diff_lambda_q2, diff_lambda_k2], axis=1).astype(F32)
    lam_init = jnp.asarray([0.8 - 0.6 * math.exp(-0.3 * l) for l in range(depth)], F32)
    cst = jnp.zeros((depth, 1, LANES), F32).at[:, 0, 0].set(lam_init)

    def row3(a):
        return a.astype(F32)[:, None, :]

    posb = jnp.broadcast_to(positions.astype(F32).reshape(t, 1), (t, LANES))
    half = MLA_ROPE // 2
    inv = 1.0 / (ROPE_THETA ** (jnp.arange(0, MLA_ROPE, 2, dtype=F32) / MLA_ROPE))
    inv = jnp.concatenate([inv, inv, jnp.zeros((LANES - 2 * half,), F32)])[None, :]
    cos, sa, sb = _rope_tables(posb, inv, tm=1024)

    blk = 512
    xs = x.reshape(t, d)
    for l in range(depth):
        proj = _norm_matmul(xs, row3(attn_norm), w_in_p, l, tm=1024, tn=1280)
        mq, mk, mv = _mla_prep(proj, row3(mla_q_norm), row3(mla_kv_norm), wq_p, wkv_p, cos, sa, sb, l, tm=512)
        diff_extras = [
            (lam_p, pl.BlockSpec((None, 4, DIFF_HD), lambda b, h, i, l=l: (l, 0, 0))),
            (cst, pl.BlockSpec((None, 1, LANES), lambda b, h, i, l=l: (l, 0, 0))),
            (row3(diff_subln), pl.BlockSpec((None, 1, DIFF_VD), lambda b, h, i, l=l: (l, 0, 0))),
        ]
        o_diff = _attention(proj, proj, proj, diff_extras, batch=batch, seq=seq, heads=DIFF_HEADS,
                            qk_dim=LANES, q_col=COL_DQ // LANES, k_col=COL_DK // LANES,
                            v_col=COL_DV // LANES, blk=blk, diff=True, name="diff_attn")
        o_gla = _gla(proj, gw_p, row3(gla_gate_b), row3(gla_norm), l, batch=batch, seq=seq, chunk=256)
        o_mla = _attention(mq, mk, mv, [], batch=batch, seq=seq, heads=MLA_HEADS,
                           qk_dim=MLA_QK, q_col=0, k_col=0, v_col=0, blk=blk, diff=False,
                           name="mla_attn")
        x1 = _out_proj(xs, o_diff, o_gla, o_mla, wo_a, wo_b, wo_c, l, tm=512)
        xs = _ffn(x1, row3(ffn_norm), wg_p, wu_p, wd_p, final_norm.astype(F32)[None, :], l,
                  tm=512, tf=512, final_norm=(l == depth - 1))
    return xs.reshape(batch, seq, d)
```

```python
import functools
import math

import jax
import jax.numpy as jnp
from jax import lax
from jax.experimental import pallas as pl
from jax.experimental.pallas import tpu as pltpu

F32 = jnp.float32
BF16 = jnp.bfloat16

D_MODEL = 2048
DIFF_HEADS = 6
DIFF_HD = 64
DIFF_VD = 128
GLA_HEADS = 4
GLA_DK = 64
GLA_DV = 128
GLA_GATE_RANK = 16
GLA_TAU = 16.0
MLA_HEADS = 6
MLA_Q_RANK = 512
MLA_KV_RANK = 256
MLA_NOPE = 128
MLA_ROPE = 64
MLA_VD = 128
ROPE_THETA = 10000.0
D_FF = 5632
EPS = 1e-6
LOG2E = 1.4426950408889634

LANES = 128
VMEM_LIMIT = 56 * 1024 * 1024

PROJ_COLS = 5120
COL_DQ, COL_DK, COL_DV = 0, 768, 1536
COL_GQ, COL_GK, COL_MKVA, COL_GV, COL_GO, COL_MQA, COL_MKR, COL_GR = (
    2304, 2560, 2816, 3072, 3584, 4096, 4608, 4672)
MLA_QK = 256
STRIP = 256
ONES_ROWS = 16


def _cparams(sem, flags=None):
    return pltpu.CompilerParams(dimension_semantics=sem, vmem_limit_bytes=VMEM_LIMIT, flags=flags)


def _rms(x, g):
    return x * lax.rsqrt(jnp.mean(x * x, axis=-1, keepdims=True) + EPS) * g


def _norm_matmul_kernel(x_ref, g_ref, w_ref, o_ref, h_ref):
    @pl.when(pl.program_id(1) == 0)
    def _():
        h_ref[...] = _rms(x_ref[...].astype(F32), g_ref[...]).astype(BF16)

    o_ref[...] = jnp.dot(h_ref[...], w_ref[...], preferred_element_type=F32).astype(o_ref.dtype)


def _norm_matmul(x, g, w, layer, *, tm, tn):
    t, d = x.shape
    n = w.shape[-1]
    return pl.pallas_call(
        _norm_matmul_kernel,
        grid=(t // tm, n // tn),
        in_specs=[
            pl.BlockSpec((tm, d), lambda i, j: (i, 0)),
            pl.BlockSpec((None, 1, d), lambda i, j: (layer, 0, 0)),
            pl.BlockSpec((None, d, tn), lambda i, j: (layer, 0, j)),
        ],
        out_specs=pl.BlockSpec((tm, tn), lambda i, j: (i, j)),
        out_shape=jax.ShapeDtypeStruct((t, n), BF16),
        scratch_shapes=[pltpu.VMEM((tm, d), BF16)],
        compiler_params=_cparams(("parallel", "arbitrary")),
        name="in_proj",
    )(x, g, w)


def _rope_table_kernel(pos_ref, inv_ref, cos_ref, sa_ref, sb_ref):
    ang = pos_ref[...] * inv_ref[...]
    c = jnp.cos(ang)
    s = jnp.sin(ang)
    lane = lax.broadcasted_iota(jnp.int32, ang.shape, 1)
    cos_ref[...] = jnp.where(lane < MLA_ROPE, c, 0.0)
    sa_ref[...] = jnp.where(lane < MLA_ROPE // 2, -s, 0.0)
    sb_ref[...] = jnp.where(lane < MLA_ROPE // 2, 0.0, jnp.where(lane < MLA_ROPE, s, 0.0))


def _rope_tables(posb, inv, *, tm):
    t = posb.shape[0]
    spec = pl.BlockSpec((tm, LANES), lambda i: (i, 0))
    return pl.pallas_call(
        _rope_table_kernel,
        grid=(t // tm,),
        in_specs=[spec, pl.BlockSpec((1, LANES), lambda i: (0, 0))],
        out_specs=[spec, spec, spec],
        out_shape=[jax.ShapeDtypeStruct((t, LANES), F32)] * 3,
        compiler_params=_cparams(("parallel",)),
        name="rope_tables",
    )(posb, inv)


def _rope128(x, cos, sa, sb):
    return x * cos + pltpu.roll(x, 96, 1) * sa + pltpu.roll(x, 32, 1) * sb


def _mla_prep_kernel(qa_ref, kva_ref, kr_ref, gq_ref, gkv_ref, wq_ref, wkv_ref,
                     cos_ref, sa_ref, sb_ref, q_ref, k_ref, v_ref):
    cos, sa, sb = cos_ref[...], sa_ref[...], sb_ref[...]
    scale = (MLA_NOPE + MLA_ROPE) ** -0.5 * LOG2E

    hq = _rms(qa_ref[...].astype(F32), gq_ref[...]).astype(BF16)
    q = jnp.dot(hq, wq_ref[...], preferred_element_type=F32)
    for h in range(MLA_HEADS):
        lo = h * MLA_QK
        q_ref[:, lo:lo + LANES] = (q[:, lo:lo + LANES] * scale).astype(BF16)
        rp = _rope128(q[:, lo + LANES:lo + 2 * LANES], cos, sa, sb)
        q_ref[:, lo + LANES:lo + 2 * LANES] = (rp * scale).astype(BF16)

    hk = _rms(kva_ref[...].astype(F32), gkv_ref[...]).astype(BF16)
    kv = jnp.dot(hk, wkv_ref[...], preferred_element_type=F32)
    lane = lax.broadcasted_iota(jnp.int32, cos.shape, 1)
    kr = jnp.where(lane < MLA_ROPE, kr_ref[...].astype(F32), 0.0)
    krr = _rope128(kr, cos, sa, sb).astype(BF16)
    for h in range(MLA_HEADS):
        lo = h * MLA_QK
        k_ref[:, lo:lo + LANES] = kv[:, h * LANES:(h + 1) * LANES].astype(BF16)
        k_ref[:, lo + LANES:lo + 2 * LANES] = krr
    v_ref[...] = kv[:, MLA_HEADS * LANES:].astype(BF16)


def _mla_prep(proj, gq, gkv, wq, wkv, cos, sa, sb, layer, *, tm):
    t = proj.shape[0]
    nq = MLA_HEADS * MLA_QK
    nv = MLA_HEADS * MLA_VD
    tab = pl.BlockSpec((tm, LANES), lambda i: (i, 0))
    return pl.pallas_call(
        _mla_prep_kernel,
        grid=(t // tm,),
        in_specs=[
            pl.BlockSpec((tm, MLA_Q_RANK), lambda i: (i, COL_MQA // MLA_Q_RANK)),
            pl.BlockSpec((tm, MLA_KV_RANK), lambda i: (i, COL_MKVA // MLA_KV_RANK)),
            pl.BlockSpec((tm, LANES), lambda i: (i, COL_MKR // LANES)),
            pl.BlockSpec((None, 1, MLA_Q_RANK), lambda i: (layer, 0, 0)),
            pl.BlockSpec((None, 1, MLA_KV_RANK), lambda i: (layer, 0, 0)),
            pl.BlockSpec((None, MLA_Q_RANK, nq), lambda i: (layer, 0, 0)),
            pl.BlockSpec((None, MLA_KV_RANK, nq), lambda i: (layer, 0, 0)),
            tab, tab, tab,
        ],
        out_specs=[
            pl.BlockSpec((tm, nq), lambda i: (i, 0)),
            pl.BlockSpec((tm, nq), lambda i: (i, 0)),
            pl.BlockSpec((tm, nv), lambda i: (i, 0)),
        ],
        out_shape=[
            jax.ShapeDtypeStruct((t, nq), BF16),
            jax.ShapeDtypeStruct((t, nq), BF16),
            jax.ShapeDtypeStruct((t, nv), BF16),
        ],
        compiler_params=_cparams(("parallel",)),
        name="mla_prep",
    )(proj, proj, proj, gq, gkv, wq, wkv, cos, sa, sb)


def _attn_kernel(*refs, n_maps, bq, bk, diff):
    if diff:
        q_ref, k_ref, v_ref, lam_ref, cst_ref, subln_ref, o_ref = refs[:7]
    else:
        q_ref, k_ref, v_ref, o_ref = refs[:4]
    qt_s, vt_s, s0, s1, p0, p1, mb0, mb1, al0, al1, m_s, acc_s = refs[-12:]
    sbuf, pbuf, mbuf, abuf = (s0, s1), (p0, p1), (mb0, mb1), (al0, al1)
    qi = pl.program_id(2)
    ratio = bq // bk
    vd = LANES

    @pl.when(qi == 0)
    def _():
        def tr(c, carry):
            off = pl.multiple_of(c * bk, bk)
            vt_s[0:vd, pl.ds(off, bk)] = v_ref[pl.ds(off, bk), :].astype(F32).T.astype(BF16)
            return carry
        lax.fori_loop(0, vt_s.shape[1] // bk, tr, 0)
        vt_s[vd:, :] = jnp.ones((vt_s.shape[0] - vd, vt_s.shape[1]), BF16)

    if diff:
        qf = q_ref[...].astype(F32) * (DIFF_HD ** -0.5 * LOG2E)
        qt = qf.T
        row = lax.broadcasted_iota(jnp.int32, qt.shape, 0)
        qt_s[:, :bq] = jnp.where(row < DIFF_HD, qt, 0.0).astype(BF16)
        qt_s[:, bq:] = jnp.where(row < DIFF_HD, 0.0, qt).astype(BF16)
    else:
        qt_s[...] = q_ref[...].astype(F32).T.astype(BF16)

    m_s[...] = jnp.full(m_s.shape, -jnp.inf, F32)
    acc_s[...] = jnp.zeros(acc_s.shape, F32)

    nq = n_maps * bq
    strips = [slice(c * STRIP, (c + 1) * STRIP) for c in range(nq // STRIP)]

    def scores(j, slot, masked, cs):
        off = pl.multiple_of(j * bk, bk)
        s = jnp.dot(k_ref[pl.ds(off, bk), :], qt_s[:, cs], preferred_element_type=F32)
        if masked:
            key = lax.broadcasted_iota(jnp.int32, s.shape, 0) + (j * bk - qi * bq)
            qry = jnp.bitwise_and(lax.broadcasted_iota(jnp.int32, s.shape, 1) + cs.start, bq - 1)
            s = jnp.where(key <= qry, s, -jnp.inf)
        sbuf[slot][:, cs] = s
        mbuf[slot][:, cs] = jnp.max(s, axis=0, keepdims=True)

    def softmax(slot, cs):
        m_old = m_s[:, cs]
        m_new = jnp.maximum(m_old, mbuf[slot][:, cs])
        alpha = jnp.exp2(m_old - m_new)
        abuf[slot][:, cs] = alpha
        p = jnp.exp2(sbuf[slot][:, cs] - m_new)
        pbuf[slot][:, cs] = p.astype(BF16)
        m_s[:, cs] = m_new

    def values(j, slot, cs):
        off = pl.multiple_of(j * bk, bk)
        upd = jnp.dot(vt_s[:, pl.ds(off, bk)], pbuf[slot][:, cs],
                      preferred_element_type=F32)
        acc_s[:, cs] = acc_s[:, cs] * abuf[slot][:, cs] + upd

    def pair(masked):
        def body(pi, carry):
            j = 2 * pi
            for cs in strips:
                softmax(1, cs)
                scores(j, 0, masked, cs)
                values(j - 2, 0, cs)
            for cs in strips:
                softmax(0, cs)
                scores(j + 1, 1, masked, cs)
                values(j - 1, 1, cs)
            return carry
        return body

    n_unmasked = qi * (ratio // 2)
    n_pairs = n_unmasked + ratio // 2
    for cs in strips:
        scores(0, 0, True, cs)
        scores(1, 1, True, cs)
        softmax(0, cs)
    lax.fori_loop(1, n_unmasked, pair(False), 0)
    lax.fori_loop(jnp.maximum(n_unmasked, 1), n_pairs, pair(True), 0)
    for cs in strips:
        values(2 * n_pairs - 2, 0, cs)
        softmax(1, cs)
        values(2 * n_pairs - 1, 1, cs)

    ot = acc_s[0:vd, :] * (1.0 / acc_s[vd:vd + 1, :])
    if diff:
        lp = lam_ref[...]
        lam_init = cst_ref[:, 0:1]
        lam = (jnp.exp(jnp.sum(lp[0:1] * lp[1:2], axis=-1, keepdims=True))
               - jnp.exp(jnp.sum(lp[2:3] * lp[3:4], axis=-1, keepdims=True)) + lam_init)
        d = (ot[:, :bq] - lam * ot[:, bq:]).T
        o_ref[...] = (_rms(d, subln_ref[...]) * (1.0 - lam_init)).astype(o_ref.dtype)
    else:
        o_ref[...] = ot.T.astype(o_ref.dtype)


def _attention(q, k, v, extras, *, batch, seq, heads, qk_dim, q_col, k_col, v_col, bq, bk, diff, name):
    n_maps = 2 if diff else 1
    nq = n_maps * bq
    nqb = seq // bq
    assert bq % (2 * bk) == 0 and seq % bq == 0
    kernel = functools.partial(_attn_kernel, n_maps=n_maps, bq=bq, bk=bk, diff=diff)
    in_specs = [
        pl.BlockSpec((bq, qk_dim), lambda b, h, i: (b * nqb + i, q_col + h)),
        pl.BlockSpec((seq, qk_dim), lambda b, h, i: (b, k_col + h)),
        pl.BlockSpec((seq, LANES), lambda b, h, i: (b, v_col + h)),
    ]
    in_specs += [spec for _, spec in extras]
    row = pltpu.VMEM((1, nq), F32)
    return pl.pallas_call(
        kernel,
        grid=(batch, heads, nqb),
        in_specs=in_specs,
        out_specs=pl.BlockSpec((bq, LANES), lambda b, h, i: (b * nqb + i, h)),
        out_shape=jax.ShapeDtypeStruct((batch * seq, heads * LANES), BF16),
        scratch_shapes=[
            pltpu.VMEM((qk_dim, nq), BF16),
            pltpu.VMEM((LANES + ONES_ROWS, seq), BF16),
            pltpu.VMEM((bk, nq), F32), pltpu.VMEM((bk, nq), F32),
            pltpu.VMEM((bk, nq), BF16), pltpu.VMEM((bk, nq), BF16),
            row, row, row, row, row,
            pltpu.VMEM((LANES + ONES_ROWS, nq), F32),
        ],
        compiler_params=_cparams(("parallel", "parallel", "arbitrary")),
        name=name,
    )(q, k, v, *[a for a, _ in extras])


def _gla_kernel(gq_ref, gk_ref, gv_ref, go_ref, gr_ref, gw_ref, gb_ref, gn_ref, o_ref, st_ref, *, chunk):
    c = chunk

    @pl.when(pl.program_id(1) == 0)
    def _():
        st_ref[...] = jnp.zeros(st_ref.shape, F32)

    logits = jnp.dot(gr_ref[...], gw_ref[...], preferred_element_type=F32) + gb_ref[...]
    la = (jnp.minimum(logits, 0.0) - jnp.log(1.0 + jnp.exp(-jnp.abs(logits)))) * (1.0 / GLA_TAU)

    row = lax.broadcasted_iota(jnp.int32, la.shape, 0)
    b = la
    s = 1
    while s < c:
        b = b + jnp.where(row >= s, pltpu.roll(b, s, 0), 0.0)
        s *= 2

    q = gq_ref[...].astype(F32) * (GLA_DK ** -0.5)
    k = gk_ref[...].astype(F32)
    v = gv_ref[...]
    lane = lax.broadcasted_iota(jnp.int32, (c, LANES), 1)
    first_head = lane < GLA_DK

    ri = lax.broadcasted_iota(jnp.int32, (c, c), 0)
    ci = lax.broadcasted_iota(jnp.int32, (c, c), 1)
    lvl = jnp.where(ri > ci, jnp.bitwise_xor(ri, ci), 0)

    def head_scores(qt, kt):
        out = []
        for p in range(GLA_HEADS // 2):
            qp = qt[:, p * LANES:(p + 1) * LANES]
            kp = kt[:, p * LANES:(p + 1) * LANES].astype(BF16)
            for hh in range(2):
                qh = jnp.where(first_head if hh == 0 else jnp.logical_not(first_head), qp, 0.0)
                out.append(lax.dot_general(qh.astype(BF16), kp, (((1,), (1,)), ((), ())),
                                           preferred_element_type=F32))
        return out

    attn = [jnp.where(ri == ci, m, 0.0) for m in head_scores(q, k)]
    ref_b = b
    hs = 1
    bit = 0
    while hs < c:
        if hs > 1:
            ref_b = jnp.where(jnp.bitwise_and(row, hs - 1) < hs // 2, ref_b,
                              pltpu.roll(ref_b, hs // 2, 0))
        qt = q * jnp.exp(jnp.minimum(b - ref_b, 0.0))
        nxt = pltpu.roll(ref_b, c - hs, 0)
        kt = k * jnp.exp(jnp.minimum(nxt - b, 0.0))
        mask = lax.shift_right_logical(lvl, bit) == 1
        attn = [jnp.where(mask, m, a) for m, a in zip(head_scores(qt, kt), attn)]
        hs *= 2
        bit += 1

    b_end = b[c - 1:c, :]
    qb = (q * jnp.exp(b)).astype(BF16)
    kd = (k * jnp.exp(b_end - b)).astype(BF16)
    er = lax.broadcasted_iota(jnp.int32, (2 * GLA_DV, LANES), 0)
    ec = lax.broadcasted_iota(jnp.int32, (2 * GLA_DV, LANES), 1)
    bd_mask = (er < GLA_DV) == (ec < GLA_DK)

    outs = []
    for p in range(GLA_HEADS // 2):
        st = st_ref[p]
        inter = lax.dot_general(qb[:, p * LANES:(p + 1) * LANES], st.astype(BF16),
                                (((1,), (1,)), ((), ())), preferred_element_type=F32)
        vp = v[:, p * 2 * GLA_DV:(p + 1) * 2 * GLA_DV]
        for hh in range(2):
            h = 2 * p + hh
            intra = jnp.dot(attn[h].astype(BF16), vp[:, hh * GLA_DV:(hh + 1) * GLA_DV],
                            preferred_element_type=F32)
            outs.append(inter[:, hh * GLA_DV:(hh + 1) * GLA_DV] + intra)
        upd = lax.dot_general(vp, kd[:, p * LANES:(p + 1) * LANES], (((0,), (0,)), ((), ())),
                              preferred_element_type=F32)
        st_ref[p] = st * jnp.exp(b_end[:, p * LANES:(p + 1) * LANES]) + jnp.where(bd_mask, upd, 0.0)

    gn = gn_ref[...]
    for h in range(GLA_HEADS):
        g = go_ref[:, h * GLA_DV:(h + 1) * GLA_DV].astype(F32)
        y = _rms(outs[h], gn) * (g / (1.0 + jnp.exp(-g)))
        o_ref[:, h * GLA_DV:(h + 1) * GLA_DV] = y.astype(o_ref.dtype)


def _gla(proj, gw, gb, gn, layer, *, batch, seq, chunk):
    nc = seq // chunk
    qk = GLA_HEADS * GLA_DK
    vd = GLA_HEADS * GLA_DV
    return pl.pallas_call(
        functools.partial(_gla_kernel, chunk=chunk),
        grid=(batch, nc),
        in_specs=[
            pl.BlockSpec((chunk, qk), lambda b, c: (b * nc + c, COL_GQ // qk)),
            pl.BlockSpec((chunk, qk), lambda b, c: (b * nc + c, COL_GK // qk)),
            pl.BlockSpec((chunk, vd), lambda b, c: (b * nc + c, COL_GV // vd)),
            pl.BlockSpec((chunk, vd), lambda b, c: (b * nc + c, COL_GO // vd)),
            pl.BlockSpec((chunk, LANES), lambda b, c: (b * nc + c, COL_MKR // LANES)),
            pl.BlockSpec((None, LANES, qk), lambda b, c: (layer, 0, 0)),
            pl.BlockSpec((None, 1, qk), lambda b, c: (layer, 0, 0)),
            pl.BlockSpec((None, 1, GLA_DV), lambda b, c: (layer, 0, 0)),
        ],
        out_specs=pl.BlockSpec((chunk, vd), lambda b, c: (b * nc + c, 0)),
        out_shape=jax.ShapeDtypeStruct((batch * seq, vd), BF16),
        scratch_shapes=[pltpu.VMEM((GLA_HEADS // 2, 2 * GLA_DV, LANES), F32)],
        compiler_params=_cparams(("parallel", "arbitrary")),
        name="gla",
    )(proj, proj, proj, proj, proj, gw, gb, gn)


def _out_proj_kernel(x_ref, a_ref, b_ref, c_ref, wa_ref, wb_ref, wc_ref, o_ref):
    acc = jnp.dot(a_ref[...], wa_ref[...], preferred_element_type=F32)
    acc += jnp.dot(b_ref[...], wb_ref[...], preferred_element_type=F32)
    acc += jnp.dot(c_ref[...], wc_ref[...], preferred_element_type=F32)
    o_ref[...] = x_ref[...] + acc


def _out_proj(x, oa, ob, oc, wa, wb, wc, layer, *, tm):
    t, d = x.shape

    def act(a):
        return pl.BlockSpec((tm, a.shape[1]), lambda i: (i, 0))

    def wgt(w):
        return pl.BlockSpec((None, w.shape[1], d), lambda i: (layer, 0, 0))

    return pl.pallas_call(
        _out_proj_kernel,
        grid=(t // tm,),
        in_specs=[act(x), act(oa), act(ob), act(oc), wgt(wa), wgt(wb), wgt(wc)],
        out_specs=pl.BlockSpec((tm, d), lambda i: (i, 0)),
        out_shape=jax.ShapeDtypeStruct((t, d), F32),
        compiler_params=_cparams(("parallel",)),
        name="out_proj",
    )(x, oa, ob, oc, wa, wb, wc)


def _ffn_kernel(x_ref, g_ref, wg_ref, wu_ref, wd_ref, fg_ref, o_ref, h_ref, *, final_norm):
    f = pl.program_id(1)

    @pl.when(f == 0)
    def _():
        x = x_ref[...]
        h_ref[...] = _rms(x, g_ref[...]).astype(BF16)
        o_ref[...] = x

    h = h_ref[...]
    gate = jnp.dot(h, wg_ref[...], preferred_element_type=F32)
    up = jnp.dot(h, wu_ref[...], preferred_element_type=F32)
    act = (gate / (1.0 + jnp.exp(-gate)) * up).astype(BF16)
    o_ref[...] += jnp.dot(act, wd_ref[...], preferred_element_type=F32)

    if final_norm:
        @pl.when(f == pl.num_programs(1) - 1)
        def _():
            o_ref[...] = _rms(o_ref[...], fg_ref[...])


def _ffn(x, g, wg, wu, wd, fg, layer, *, tm, tf, final_norm):
    t, d = x.shape
    ff = wg.shape[-1]
    return pl.pallas_call(
        functools.partial(_ffn_kernel, final_norm=final_norm),
        grid=(t // tm, ff // tf),
        in_specs=[
            pl.BlockSpec((tm, d), lambda i, f: (i, 0)),
            pl.BlockSpec((None, 1, d), lambda i, f: (layer, 0, 0)),
            pl.BlockSpec((None, d, tf), lambda i, f: (layer, 0, f)),
            pl.BlockSpec((None, d, tf), lambda i, f: (layer, 0, f)),
            pl.BlockSpec((None, tf, d), lambda i, f: (layer, f, 0)),
            pl.BlockSpec((1, d), lambda i, f: (0, 0)),
        ],
        out_specs=pl.BlockSpec((tm, d), lambda i, f: (i, 0)),
        out_shape=jax.ShapeDtypeStruct((t, d), F32),
        scratch_shapes=[pltpu.VMEM((tm, d), BF16)],
        compiler_params=_cparams(("parallel", "arbitrary")),
        name="ffn",
    )(x, g, wg, wu, wd, fg)


def _prep_w_in(w_in):
    sizes = (768, 768, 768, 256, 256, 512, 16, 512, 512, 256, 64)
    offs = [0]
    for s in sizes:
        offs.append(offs[-1] + s)
    d_q, d_k, d_v, g_q, g_k, g_v, g_r, g_o, m_qa, m_kva, m_kr = (
        w_in[..., offs[i]:offs[i + 1]] for i in range(len(sizes)))
    used = COL_GR + GLA_GATE_RANK
    pad = jnp.zeros(w_in.shape[:-1] + (PROJ_COLS - used,), w_in.dtype)
    out = jnp.concatenate([d_q, d_k, d_v, g_q, g_k, m_kva, g_v, g_o, m_qa, m_kr, g_r, pad], axis=-1)
    return out.astype(BF16)


def _prep_w_q_b(w):
    depth, r, _ = w.shape
    w = w.reshape(depth, r, MLA_HEADS, MLA_NOPE + MLA_ROPE)
    w = jnp.pad(w, ((0, 0), (0, 0), (0, 0), (0, MLA_QK - MLA_NOPE - MLA_ROPE)))
    return w.reshape(depth, r, MLA_HEADS * MLA_QK).astype(BF16)


def _prep_w_kv_b(w):
    depth, r, _ = w.shape
    w = w.reshape(depth, r, MLA_HEADS, MLA_NOPE + MLA_VD)
    k = w[..., :MLA_NOPE].reshape(depth, r, MLA_HEADS * MLA_NOPE)
    v = w[..., MLA_NOPE:].reshape(depth, r, MLA_HEADS * MLA_VD)
    return jnp.concatenate([k, v], axis=-1).astype(BF16)


def _prep_gate_w(w):
    depth = w.shape[0]
    lo = COL_GR - COL_MKR
    out = jnp.zeros((depth, LANES, w.shape[-1]), w.dtype)
    return out.at[:, lo:lo + GLA_GATE_RANK, :].set(w).astype(BF16)


def kernel(x, positions, attn_norm, w_in, diff_lambda_q1, diff_lambda_k1, diff_lambda_q2, diff_lambda_k2, diff_subln, gla_gate_w, gla_gate_b, gla_norm, mla_q_norm, mla_w_q_b, mla_kv_norm, mla_w_kv_b, w_out, ffn_norm, w_gate, w_up, w_down, final_norm):
    batch, seq, d = x.shape
    depth = w_in.shape[0]
    t = batch * seq
    assert d == D_MODEL and seq % 1024 == 0

    w_in_p = _prep_w_in(w_in)
    wq_p = _prep_w_q_b(mla_w_q_b)
    wkv_p = _prep_w_kv_b(mla_w_kv_b)
    gw_p = _prep_gate_w(gla_gate_w)
    n_a, n_b = DIFF_HEADS * DIFF_VD, GLA_HEADS * GLA_DV
    wo_a = w_out[:, :n_a].astype(BF16)
    wo_b = w_out[:, n_a:n_a + n_b].astype(BF16)
    wo_c = w_out[:, n_a + n_b:].astype(BF16)
    wg_p, wu_p, wd_p = w_gate.astype(BF16), w_up.astype(BF16), w_down.astype(BF16)
    lam_p = jnp.stack([diff_lambda_q1, diff_lambda_k1, diff_lambda_q2, diff_lambda_k2], axis=1).astype(F32)
    lam_init = jnp.asarray([0.8 - 0.6 * math.exp(-0.3 * l) for l in range(depth)], F32)
    cst = jnp.zeros((depth, 1, LANES), F32).at[:, 0, 0].set(lam_init)

    def row3(a):
        return a.astype(F32)[:, None, :]

    posb = jnp.broadcast_to(positions.astype(F32).reshape(t, 1), (t, LANES))
    half = MLA_ROPE // 2
    inv = 1.0 / (ROPE_THETA ** (jnp.arange(0, MLA_ROPE, 2, dtype=F32) / MLA_ROPE))
    inv = jnp.concatenate([inv, inv, jnp.zeros((LANES - 2 * half,), F32)])[None, :]
    cos, sa, sb = _rope_tables(posb, inv, tm=1024)

    xs = x.reshape(t, d)
    for l in range(depth):
        proj = _norm_matmul(xs, row3(attn_norm), w_in_p, l, tm=1024, tn=1280)
        mq, mk, mv = _mla_prep(proj, row3(mla_q_norm), row3(mla_kv_norm), wq_p, wkv_p, cos, sa, sb, l, tm=512)
        diff_extras = [
            (lam_p, pl.BlockSpec((None, 4, DIFF_HD), lambda b, h, i, l=l: (l, 0, 0))),
            (cst, pl.BlockSpec((None, 1, LANES), lambda b, h, i, l=l: (l, 0, 0))),
            (row3(diff_subln), pl.BlockSpec((None, 1, DIFF_VD), lambda b, h, i, l=l: (l, 0, 0))),
        ]
        o_diff = _attention(proj, proj, proj, diff_extras, batch=batch, seq=seq, heads=DIFF_HEADS,
                            qk_dim=LANES, q_col=COL_DQ // LANES, k_col=COL_DK // LANES,
                            v_col=COL_DV // LANES, bq=1024, bk=512, diff=True, name="diff_attn")
        o_gla = _gla(proj, gw_p, row3(gla_gate_b), row3(gla_norm), l, batch=batch, seq=seq, chunk=256)
        o_mla = _attention(mq, mk, mv, [], batch=batch, seq=seq, heads=MLA_HEADS,
                           qk_dim=MLA_QK, q_col=0, k_col=0, v_col=0, bq=1024, bk=512, diff=False,
                           name="mla_attn")
        x1 = _out_proj(xs, o_diff, o_gla, o_mla, wo_a, wo_b, wo_c, l, tm=512)
        xs = _ffn(x1, row3(ffn_norm), wg_p, wu_p, wd_p, final_norm.astype(F32)[None, :], l,
                  tm=512, tf=512, final_norm=(l == depth - 1))
    return xs.reshape(batch, seq, d)
```

```python
import functools
import math

import jax
import jax.numpy as jnp
from jax import lax
from jax.experimental import pallas as pl
from jax.experimental.pallas import tpu as pltpu

F32 = jnp.float32
BF16 = jnp.bfloat16

D_MODEL = 2048
DIFF_HEADS = 6
DIFF_HD = 64
DIFF_VD = 128
GLA_HEADS = 4
GLA_DK = 64
GLA_DV = 128
GLA_GATE_RANK = 16
GLA_TAU = 16.0
MLA_HEADS = 6
MLA_Q_RANK = 512
MLA_KV_RANK = 256
MLA_NOPE = 128
MLA_ROPE = 64
MLA_VD = 128
ROPE_THETA = 10000.0
D_FF = 5632
EPS = 1e-6
LOG2E = 1.4426950408889634

LANES = 128
VMEM_LIMIT = 56 * 1024 * 1024

PROJ_COLS = 5120
COL_DQ, COL_DK, COL_DV = 0, 768, 1536
COL_GQ, COL_GK, COL_MKVA, COL_GV, COL_GO, COL_MQA, COL_MKR, COL_GR = (
    2304, 2560, 2816, 3072, 3584, 4096, 4608, 4672)
MLA_QK = 256
STRIP = 256
ONES_ROWS = 16
GAP_LIMIT = 64.0


def _cparams(sem, flags=None):
    return pltpu.CompilerParams(dimension_semantics=sem, vmem_limit_bytes=VMEM_LIMIT, flags=flags)


def _rms(x, g):
    return x * lax.rsqrt(jnp.mean(x * x, axis=-1, keepdims=True) + EPS) * g


def _norm_matmul_kernel(x_ref, g_ref, w_ref, o_ref, h_ref):
    @pl.when(pl.program_id(1) == 0)
    def _():
        h_ref[...] = _rms(x_ref[...].astype(F32), g_ref[...]).astype(BF16)

    o_ref[...] = jnp.dot(h_ref[...], w_ref[...], preferred_element_type=F32).astype(o_ref.dtype)


def _norm_matmul(x, g, w, layer, *, tm, tn):
    t, d = x.shape
    n = w.shape[-1]
    return pl.pallas_call(
        _norm_matmul_kernel,
        grid=(t // tm, n // tn),
        in_specs=[
            pl.BlockSpec((tm, d), lambda i, j: (i, 0)),
            pl.BlockSpec((None, 1, d), lambda i, j: (layer, 0, 0)),
            pl.BlockSpec((None, d, tn), lambda i, j: (layer, 0, j)),
        ],
        out_specs=pl.BlockSpec((tm, tn), lambda i, j: (i, j)),
        out_shape=jax.ShapeDtypeStruct((t, n), BF16),
        scratch_shapes=[pltpu.VMEM((tm, d), BF16)],
        compiler_params=_cparams(("parallel", "arbitrary")),
        name="in_proj",
    )(x, g, w)


def _rope_table_kernel(pos_ref, inv_ref, cos_ref, sa_ref, sb_ref):
    ang = pos_ref[...] * inv_ref[...]
    c = jnp.cos(ang)
    s = jnp.sin(ang)
    lane = lax.broadcasted_iota(jnp.int32, ang.shape, 1)
    cos_ref[...] = jnp.where(lane < MLA_ROPE, c, 0.0)
    sa_ref[...] = jnp.where(lane < MLA_ROPE // 2, -s, 0.0)
    sb_ref[...] = jnp.where(lane < MLA_ROPE // 2, 0.0, jnp.where(lane < MLA_ROPE, s, 0.0))


def _rope_tables(posb, inv, *, tm):
    t = posb.shape[0]
    spec = pl.BlockSpec((tm, LANES), lambda i: (i, 0))
    return pl.pallas_call(
        _rope_table_kernel,
        grid=(t // tm,),
        in_specs=[spec, pl.BlockSpec((1, LANES), lambda i: (0, 0))],
        out_specs=[spec, spec, spec],
        out_shape=[jax.ShapeDtypeStruct((t, LANES), F32)] * 3,
        compiler_params=_cparams(("parallel",)),
        name="rope_tables",
    )(posb, inv)


def _rope128(x, cos, sa, sb):
    return x * cos + pltpu.roll(x, 96, 1) * sa + pltpu.roll(x, 32, 1) * sb


def _mla_prep_kernel(qa_ref, kva_ref, kr_ref, gq_ref, gkv_ref, wq_ref, wkv_ref,
                     cos_ref, sa_ref, sb_ref, q_ref, k_ref, v_ref):
    cos, sa, sb = cos_ref[...], sa_ref[...], sb_ref[...]
    scale = (MLA_NOPE + MLA_ROPE) ** -0.5 * LOG2E

    hq = _rms(qa_ref[...].astype(F32), gq_ref[...]).astype(BF16)
    q = jnp.dot(hq, wq_ref[...], preferred_element_type=F32)
    for h in range(MLA_HEADS):
        lo = h * MLA_QK
        q_ref[:, lo:lo + LANES] = (q[:, lo:lo + LANES] * scale).astype(BF16)
        rp = _rope128(q[:, lo + LANES:lo + 2 * LANES], cos, sa, sb)
        q_ref[:, lo + LANES:lo + 2 * LANES] = (rp * scale).astype(BF16)

    hk = _rms(kva_ref[...].astype(F32), gkv_ref[...]).astype(BF16)
    kv = jnp.dot(hk, wkv_ref[...], preferred_element_type=F32)
    lane = lax.broadcasted_iota(jnp.int32, cos.shape, 1)
    kr = jnp.where(lane < MLA_ROPE, kr_ref[...].astype(F32), 0.0)
    krr = _rope128(kr, cos, sa, sb).astype(BF16)
    for h in range(MLA_HEADS):
        lo = h * MLA_QK
        k_ref[:, lo:lo + LANES] = kv[:, h * LANES:(h + 1) * LANES].astype(BF16)
        k_ref[:, lo + LANES:lo + 2 * LANES] = krr
    v_ref[...] = kv[:, MLA_HEADS * LANES:].astype(BF16)


def _mla_prep(proj, gq, gkv, wq, wkv, cos, sa, sb, layer, *, tm):
    t = proj.shape[0]
    nq = MLA_HEADS * MLA_QK
    nv = MLA_HEADS * MLA_VD
    tab = pl.BlockSpec((tm, LANES), lambda i: (i, 0))
    return pl.pallas_call(
        _mla_prep_kernel,
        grid=(t // tm,),
        in_specs=[
            pl.BlockSpec((tm, MLA_Q_RANK), lambda i: (i, COL_MQA // MLA_Q_RANK)),
            pl.BlockSpec((tm, MLA_KV_RANK), lambda i: (i, COL_MKVA // MLA_KV_RANK)),
            pl.BlockSpec((tm, LANES), lambda i: (i, COL_MKR // LANES)),
            pl.BlockSpec((None, 1, MLA_Q_RANK), lambda i: (layer, 0, 0)),
            pl.BlockSpec((None, 1, MLA_KV_RANK), lambda i: (layer, 0, 0)),
            pl.BlockSpec((None, MLA_Q_RANK, nq), lambda i: (layer, 0, 0)),
            pl.BlockSpec((None, MLA_KV_RANK, nq), lambda i: (layer, 0, 0)),
            tab, tab, tab,
        ],
        out_specs=[
            pl.BlockSpec((tm, nq), lambda i: (i, 0)),
            pl.BlockSpec((tm, nq), lambda i: (i, 0)),
            pl.BlockSpec((tm, nv), lambda i: (i, 0)),
        ],
        out_shape=[
            jax.ShapeDtypeStruct((t, nq), BF16),
            jax.ShapeDtypeStruct((t, nq), BF16),
            jax.ShapeDtypeStruct((t, nv), BF16),
        ],
        compiler_params=_cparams(("parallel",)),
        name="mla_prep",
    )(proj, proj, proj, gq, gkv, wq, wkv, cos, sa, sb)


def _attn_kernel(*refs, n_maps, bq, bk, diff, gap_limit):
    if diff:
        q_ref, k_ref, v_ref, lam_ref, cst_ref, subln_ref, o_ref = refs[:7]
    else:
        q_ref, k_ref, v_ref, o_ref = refs[:4]
    qt_s, vt_s, p0, p1, al0, al1, r_s, over_s, acc_s = refs[-9:]
    pbuf, abuf = (p0, p1), (al0, al1)
    qi = pl.program_id(2)
    vd = LANES

    @pl.when(qi == 0)
    def _():
        def tr(c, carry):
            off = pl.multiple_of(c * bk, bk)
            vt_s[0:vd, pl.ds(off, bk)] = v_ref[pl.ds(off, bk), :].astype(F32).T.astype(BF16)
            return carry
        lax.fori_loop(0, vt_s.shape[1] // bk, tr, 0)
        vt_s[vd:, :] = jnp.ones((vt_s.shape[0] - vd, vt_s.shape[1]), BF16)

    if diff:
        qf = q_ref[...].astype(F32) * (DIFF_HD ** -0.5 * LOG2E)
        qt = qf.T
        row = lax.broadcasted_iota(jnp.int32, qt.shape, 0)
        qt_s[:, :bq] = jnp.where(row < DIFF_HD, qt, 0.0).astype(BF16)
        qt_s[:, bq:] = jnp.where(row < DIFF_HD, 0.0, qt).astype(BF16)
    else:
        qt_s[...] = q_ref[...].astype(F32).T.astype(BF16)

    acc_s[...] = jnp.zeros(acc_s.shape, F32)

    nq = n_maps * bq
    strips = [slice(c * STRIP, (c + 1) * STRIP) for c in range(nq // STRIP)]
    ratio = bq // bk
    first_diag = ratio * qi
    n_blocks = first_diag + ratio

    def scores(j, masked, cs, rows=bk):
        off = pl.multiple_of(j * bk, bk)
        s = jnp.dot(k_ref[pl.ds(off, rows), :], qt_s[:, cs], preferred_element_type=F32)
        if masked:
            key = lax.broadcasted_iota(jnp.int32, s.shape, 0) + (j * bk - qi * bq)
            qry = jnp.bitwise_and(lax.broadcasted_iota(jnp.int32, s.shape, 1) + cs.start, bq - 1)
            s = jnp.where(key <= qry, s, -jnp.inf)
        return s

    def first_reference(cs):
        mb = jnp.max(scores(0, True, cs, rows=ONES_ROWS), axis=0, keepdims=True)
        r_s[:, cs] = mb
        abuf[0][:, cs] = jnp.ones_like(mb)
        abuf[1][:, cs] = jnp.ones_like(mb)
        over_s[:, cs] = jnp.zeros_like(mb)

    def probs(j, slot, masked, cs):
        s = scores(j, masked, cs)
        r_old = r_s[:, cs]
        pbuf[slot][:, cs] = jnp.exp2(s - r_old).astype(BF16)
        mb = jnp.max(s, axis=0, keepdims=True)
        r_new = jnp.maximum(r_old, mb)
        abuf[1 - slot][:, cs] = jnp.exp2(r_old - r_new)
        over_s[:, cs] = jnp.maximum(over_s[:, cs], mb - r_old)
        r_s[:, cs] = r_new

    def values(j, slot, cs):
        off = pl.multiple_of(j * bk, bk)
        upd = jnp.dot(vt_s[:, pl.ds(off, bk)], pbuf[slot][:, cs],
                      preferred_element_type=F32)
        acc_s[:, cs] = acc_s[:, cs] * abuf[slot][:, cs] + upd

    def body(t, carry):
        j = 2 * t
        for cs in strips:
            values(j, 0, cs)
            probs(j + 1, 1, False, cs)
        for cs in strips:
            values(j + 1, 1, cs)
            probs(j + 2, 0, False, cs)
        return carry

    def visibility(d, cs):
        lo = cs.start % bq
        if lo + STRIP <= d * bk:
            return "none"
        return "all" if lo >= (d + 1) * bk else "part"

    def diag_probs(d, cs):
        if visibility(d, cs) != "none":
            probs(first_diag + d, d % 2, visibility(d, cs) == "part", cs)

    def diag_values(d, cs):
        if visibility(d, cs) != "none":
            values(first_diag + d, d % 2, cs)

    for cs in strips:
        first_reference(cs)
    for cs in strips:
        probs(0, 0, True, cs)
    lax.fori_loop(0, first_diag // 2 - 1, body, 0)

    @pl.when(qi >= 1)
    def _():
        for cs in strips:
            values(first_diag - 2, 0, cs)
            probs(first_diag - 1, 1, False, cs)
        for cs in strips:
            values(first_diag - 1, 1, cs)
            diag_probs(0, cs)

    for d in range(1, ratio):
        for cs in strips:
            diag_values(d - 1, cs)
            diag_probs(d, cs)
    for cs in strips:
        diag_values(ratio - 1, cs)

    @pl.when(jnp.max(over_s[...]) > gap_limit)
    def _():
        r_s[...] = jnp.full(r_s.shape, -jnp.inf, F32)
        acc_s[...] = jnp.zeros(acc_s.shape, F32)

        def exact(j, carry):
            off = pl.multiple_of(j * bk, bk)
            for cs in strips:
                s = scores(j, True, cs)
                m_old = r_s[:, cs]
                m_new = jnp.maximum(m_old, jnp.max(s, axis=0, keepdims=True))
                p = jnp.exp2(s - m_new).astype(BF16)
                upd = jnp.dot(vt_s[:, pl.ds(off, bk)], p, preferred_element_type=F32)
                acc_s[:, cs] = acc_s[:, cs] * jnp.exp2(m_old - m_new) + upd
                r_s[:, cs] = m_new
            return carry

        lax.fori_loop(0, n_blocks, exact, 0)

    ot = acc_s[0:vd, :] * (1.0 / acc_s[vd:vd + 1, :])
    if diff:
        lp = lam_ref[...]
        lam_init = cst_ref[:, 0:1]
        lam = (jnp.exp(jnp.sum(lp[0:1] * lp[1:2], axis=-1, keepdims=True))
               - jnp.exp(jnp.sum(lp[2:3] * lp[3:4], axis=-1, keepdims=True)) + lam_init)
        d = (ot[:, :bq] - lam * ot[:, bq:]).T
        o_ref[...] = (_rms(d, subln_ref[...]) * (1.0 - lam_init)).astype(o_ref.dtype)
    else:
        o_ref[...] = ot.T.astype(o_ref.dtype)


def _attention(q, k, v, extras, *, batch, seq, heads, qk_dim, q_col, k_col, v_col, bq, bk, diff, name,
               gap_limit=GAP_LIMIT):
    n_maps = 2 if diff else 1
    nq = n_maps * bq
    nqb = seq // bq
    assert bq % (2 * bk) == 0 and bk % STRIP == 0 and seq % bq == 0
    kernel = functools.partial(_attn_kernel, n_maps=n_maps, bq=bq, bk=bk, diff=diff, gap_limit=gap_limit)
    in_specs = [
        pl.BlockSpec((bq, qk_dim), lambda b, h, i: (b * nqb + i, q_col + h)),
        pl.BlockSpec((seq, qk_dim), lambda b, h, i: (b, k_col + h)),
        pl.BlockSpec((seq, LANES), lambda b, h, i: (b, v_col + h)),
    ]
    in_specs += [spec for _, spec in extras]
    row = pltpu.VMEM((1, nq), F32)
    return pl.pallas_call(
        kernel,
        grid=(batch, heads, nqb),
        in_specs=in_specs,
        out_specs=pl.BlockSpec((bq, LANES), lambda b, h, i: (b * nqb + i, h)),
        out_shape=jax.ShapeDtypeStruct((batch * seq, heads * LANES), BF16),
        scratch_shapes=[
            pltpu.VMEM((qk_dim, nq), BF16),
            pltpu.VMEM((LANES + ONES_ROWS, seq), BF16),
            pltpu.VMEM((bk, nq), BF16), pltpu.VMEM((bk, nq), BF16),
            row, row, row, row,
            pltpu.VMEM((LANES + ONES_ROWS, nq), F32),
        ],
        compiler_params=_cparams(("parallel", "parallel", "arbitrary")),
        name=name,
    )(q, k, v, *[a for a, _ in extras])


def _gla_kernel(gq_ref, gk_ref, gv_ref, go_ref, gr_ref, gw_ref, gb_ref, gn_ref, o_ref, st_ref, *, chunk):
    c = chunk

    @pl.when(pl.program_id(1) == 0)
    def _():
        st_ref[...] = jnp.zeros(st_ref.shape, F32)

    logits = jnp.dot(gr_ref[...], gw_ref[...], preferred_element_type=F32) + gb_ref[...]
    la = (jnp.minimum(logits, 0.0) - jnp.log(1.0 + jnp.exp(-jnp.abs(logits)))) * (1.0 / GLA_TAU)

    row = lax.broadcasted_iota(jnp.int32, la.shape, 0)
    b = la
    s = 1
    while s < c:
        b = b + jnp.where(row >= s, pltpu.roll(b, s, 0), 0.0)
        s *= 2

    q = gq_ref[...].astype(F32) * (GLA_DK ** -0.5)
    k = gk_ref[...].astype(F32)
    v = gv_ref[...]
    lane = lax.broadcasted_iota(jnp.int32, (c, LANES), 1)
    first_head = lane < GLA_DK

    ri = lax.broadcasted_iota(jnp.int32, (c, c), 0)
    ci = lax.broadcasted_iota(jnp.int32, (c, c), 1)
    lvl = jnp.where(ri > ci, jnp.bitwise_xor(ri, ci), 0)

    def head_scores(qt, kt):
        out = []
        for p in range(GLA_HEADS // 2):
            qp = qt[:, p * LANES:(p + 1) * LANES]
            kp = kt[:, p * LANES:(p + 1) * LANES].astype(BF16)
            for hh in range(2):
                qh = jnp.where(first_head if hh == 0 else jnp.logical_not(first_head), qp, 0.0)
                out.append(lax.dot_general(qh.astype(BF16), kp, (((1,), (1,)), ((), ())),
                                           preferred_element_type=F32))
        return out

    attn = [jnp.where(ri == ci, m, 0.0) for m in head_scores(q, k)]
    ref_b = b
    hs = 1
    bit = 0
    while hs < c:
        if hs > 1:
            ref_b = jnp.where(jnp.bitwise_and(row, hs - 1) < hs // 2, ref_b,
                              pltpu.roll(ref_b, hs // 2, 0))
        qt = q * jnp.exp(jnp.minimum(b - ref_b, 0.0))
        nxt = pltpu.roll(ref_b, c - hs, 0)
        kt = k * jnp.exp(jnp.minimum(nxt - b, 0.0))
        mask = lax.shift_right_logical(lvl, bit) == 1
        attn = [jnp.where(mask, m, a) for m, a in zip(head_scores(qt, kt), attn)]
        hs *= 2
        bit += 1

    b_end = b[c - 1:c, :]
    qb = (q * jnp.exp(b)).astype(BF16)
    kd = (k * jnp.exp(b_end - b)).astype(BF16)
    er = lax.broadcasted_iota(jnp.int32, (2 * GLA_DV, LANES), 0)
    ec = lax.broadcasted_iota(jnp.int32, (2 * GLA_DV, LANES), 1)
    bd_mask = (er < GLA_DV) == (ec < GLA_DK)

    outs = []
    for p in range(GLA_HEADS // 2):
        st = st_ref[p]
        inter = lax.dot_general(qb[:, p * LANES:(p + 1) * LANES], st.astype(BF16),
                                (((1,), (1,)), ((), ())), preferred_element_type=F32)
        vp = v[:, p * 2 * GLA_DV:(p + 1) * 2 * GLA_DV]
        for hh in range(2):
            h = 2 * p + hh
            intra = jnp.dot(attn[h].astype(BF16), vp[:, hh * GLA_DV:(hh + 1) * GLA_DV],
                            preferred_element_type=F32)
            outs.append(inter[:, hh * GLA_DV:(hh + 1) * GLA_DV] + intra)
        upd = lax.dot_general(vp, kd[:, p * LANES:(p + 1) * LANES], (((0,), (0,)), ((), ())),
                              preferred_element_type=F32)
        st_ref[p] = st * jnp.exp(b_end[:, p * LANES:(p + 1) * LANES]) + jnp.where(bd_mask, upd, 0.0)

    gn = gn_ref[...]
    for h in range(GLA_HEADS):
        g = go_ref[:, h * GLA_DV:(h + 1) * GLA_DV].astype(F32)
        y = _rms(outs[h], gn) * (g / (1.0 + jnp.exp(-g)))
        o_ref[:, h * GLA_DV:(h + 1) * GLA_DV] = y.astype(o_ref.dtype)


def _gla(proj, gw, gb, gn, layer, *, batch, seq, chunk):
    nc = seq // chunk
    qk = GLA_HEADS * GLA_DK
    vd = GLA_HEADS * GLA_DV
    return pl.pallas_call(
        functools.partial(_gla_kernel, chunk=chunk),
        grid=(batch, nc),
        in_specs=[
            pl.BlockSpec((chunk, qk), lambda b, c: (b * nc + c, COL_GQ // qk)),
            pl.BlockSpec((chunk, qk), lambda b, c: (b * nc + c, COL_GK // qk)),
            pl.BlockSpec((chunk, vd), lambda b, c: (b * nc + c, COL_GV // vd)),
            pl.BlockSpec((chunk, vd), lambda b, c: (b * nc + c, COL_GO // vd)),
            pl.BlockSpec((chunk, LANES), lambda b, c: (b * nc + c, COL_MKR // LANES)),
            pl.BlockSpec((None, LANES, qk), lambda b, c: (layer, 0, 0)),
            pl.BlockSpec((None, 1, qk), lambda b, c: (layer, 0, 0)),
            pl.BlockSpec((None, 1, GLA_DV), lambda b, c: (layer, 0, 0)),
        ],
        out_specs=pl.BlockSpec((chunk, vd), lambda b, c: (b * nc + c, 0)),
        out_shape=jax.ShapeDtypeStruct((batch * seq, vd), BF16),
        scratch_shapes=[pltpu.VMEM((GLA_HEADS // 2, 2 * GLA_DV, LANES), F32)],
        compiler_params=_cparams(("parallel", "arbitrary")),
        name="gla",
    )(proj, proj, proj, proj, proj, gw, gb, gn)


def _out_proj_kernel(x_ref, a_ref, b_ref, c_ref, wa_ref, wb_ref, wc_ref, o_ref):
    acc = jnp.dot(a_ref[...], wa_ref[...], preferred_element_type=F32)
    acc += jnp.dot(b_ref[...], wb_ref[...], preferred_element_type=F32)
    acc += jnp.dot(c_ref[...], wc_ref[...], preferred_element_type=F32)
    o_ref[...] = x_ref[...] + acc


def _out_proj(x, oa, ob, oc, wa, wb, wc, layer, *, tm):
    t, d = x.shape

    def act(a):
        return pl.BlockSpec((tm, a.shape[1]), lambda i: (i, 0))

    def wgt(w):
        return pl.BlockSpec((None, w.shape[1], d), lambda i: (layer, 0, 0))

    return pl.pallas_call(
        _out_proj_kernel,
        grid=(t // tm,),
        in_specs=[act(x), act(oa), act(ob), act(oc), wgt(wa), wgt(wb), wgt(wc)],
        out_specs=pl.BlockSpec((tm, d), lambda i: (i, 0)),
        out_shape=jax.ShapeDtypeStruct((t, d), F32),
        compiler_params=_cparams(("parallel",)),
        name="out_proj",
    )(x, oa, ob, oc, wa, wb, wc)


def _ffn_kernel(x_ref, g_ref, wg_ref, wu_ref, wd_ref, fg_ref, o_ref, h_ref, *, final_norm):
    f = pl.program_id(1)

    @pl.when(f == 0)
    def _():
        x = x_ref[...]
        h_ref[...] = _rms(x, g_ref[...]).astype(BF16)
        o_ref[...] = x

    h = h_ref[...]
    gate = jnp.dot(h, wg_ref[...], preferred_element_type=F32)
    up = jnp.dot(h, wu_ref[...], preferred_element_type=F32)
    act = (gate / (1.0 + jnp.exp(-gate)) * up).astype(BF16)
    o_ref[...] += jnp.dot(act, wd_ref[...], preferred_element_type=F32)

    if final_norm:
        @pl.when(f == pl.num_programs(1) - 1)
        def _():
            o_ref[...] = _rms(o_ref[...], fg_ref[...])


def _ffn(x, g, wg, wu, wd, fg, layer, *, tm, tf, final_norm):
    t, d = x.shape
    ff = wg.shape[-1]
    return pl.pallas_call(
        functools.partial(_ffn_kernel, final_norm=final_norm),
        grid=(t // tm, ff // tf),
        in_specs=[
            pl.BlockSpec((tm, d), lambda i, f: (i, 0)),
            pl.BlockSpec((None, 1, d), lambda i, f: (layer, 0, 0)),
            pl.BlockSpec((None, d, tf), lambda i, f: (layer, 0, f)),
            pl.BlockSpec((None, d, tf), lambda i, f: (layer, 0, f)),
            pl.BlockSpec((None, tf, d), lambda i, f: (layer, f, 0)),
            pl.BlockSpec((1, d), lambda i, f: (0, 0)),
        ],
        out_specs=pl.BlockSpec((tm, d), lambda i, f: (i, 0)),
        out_shape=jax.ShapeDtypeStruct((t, d), F32),
        scratch_shapes=[pltpu.VMEM((tm, d), BF16)],
        compiler_params=_cparams(("parallel", "arbitrary")),
        name="ffn",
    )(x, g, wg, wu, wd, fg)


def _prep_w_in(w_in):
    sizes = (768, 768, 768, 256, 256, 512, 16, 512, 512, 256, 64)
    offs = [0]
    for s in sizes:
        offs.append(offs[-1] + s)
    d_q, d_k, d_v, g_q, g_k, g_v, g_r, g_o, m_qa, m_kva, m_kr = (
        w_in[..., offs[i]:offs[i + 1]] for i in range(len(sizes)))
    used = COL_GR + GLA_GATE_RANK
    pad = jnp.zeros(w_in.shape[:-1] + (PROJ_COLS - used,), w_in.dtype)
    out = jnp.concatenate([d_q, d_k, d_v, g_q, g_k, m_kva, g_v, g_o, m_qa, m_kr, g_r, pad], axis=-1)
    return out.astype(BF16)


def _prep_w_q_b(w):
    depth, r, _ = w.shape
    w = w.reshape(depth, r, MLA_HEADS, MLA_NOPE + MLA_ROPE)
    w = jnp.pad(w, ((0, 0), (0, 0), (0, 0), (0, MLA_QK - MLA_NOPE - MLA_ROPE)))
    return w.reshape(depth, r, MLA_HEADS * MLA_QK).astype(BF16)


def _prep_w_kv_b(w):
    depth, r, _ = w.shape
    w = w.reshape(depth, r, MLA_HEADS, MLA_NOPE + MLA_VD)
    k = w[..., :MLA_NOPE].reshape(depth, r, MLA_HEADS * MLA_NOPE)
    v = w[..., MLA_NOPE:].reshape(depth, r, MLA_HEADS * MLA_VD)
    return jnp.concatenate([k, v], axis=-1).astype(BF16)


def _prep_gate_w(w):
    depth = w.shape[0]
    lo = COL_GR - COL_MKR
    out = jnp.zeros((depth, LANES, w.shape[-1]), w.dtype)
    return out.at[:, lo:lo + GLA_GATE_RANK, :].set(w).astype(BF16)


def kernel(x, positions, attn_norm, w_in, diff_lambda_q1, diff_lambda_k1, diff_lambda_q2, diff_lambda_k2, diff_subln, gla_gate_w, gla_gate_b, gla_norm, mla_q_norm, mla_w_q_b, mla_kv_norm, mla_w_kv_b, w_out, ffn_norm, w_gate, w_up, w_down, final_norm):
    batch, seq, d = x.shape
    depth = w_in.shape[0]
    t = batch * seq
    assert d == D_MODEL and seq % 2048 == 0

    w_in_p = _prep_w_in(w_in)
    wq_p = _prep_w_q_b(mla_w_q_b)
    wkv_p = _prep_w_kv_b(mla_w_kv_b)
    gw_p = _prep_gate_w(gla_gate_w)
    n_a, n_b = DIFF_HEADS * DIFF_VD, GLA_HEADS * GLA_DV
    wo_a = w_out[:, :n_a].astype(BF16)
    wo_b = w_out[:, n_a:n_a + n_b].astype(BF16)
    wo_c = w_out[:, n_a + n_b:].astype(BF16)
    wg_p, wu_p, wd_p = w_gate.astype(BF16), w_up.astype(BF16), w_down.astype(BF16)
    lam_p = jnp.stack([diff_lambda_q1, diff_lambda_k1, diff_lambda_q2, diff_lambda_k2], axis=1).astype(F32)
    lam_init = jnp.asarray([0.8 - 0.6 * math.exp(-0.3 * l) for l in range(depth)], F32)
    cst = jnp.zeros((depth, 1, LANES), F32).at[:, 0, 0].set(lam_init)

    def row3(a):
        return a.astype(F32)[:, None, :]

    posb = jnp.broadcast_to(positions.astype(F32).reshape(t, 1), (t, LANES))
    half = MLA_ROPE // 2
    inv = 1.0 / (ROPE_THETA ** (jnp.arange(0, MLA_ROPE, 2, dtype=F32) / MLA_ROPE))
    inv = jnp.concatenate([inv, inv, jnp.zeros((LANES - 2 * half,), F32)])[None, :]
    cos, sa, sb = _rope_tables(posb, inv, tm=1024)

    xs = x.reshape(t, d)
    for l in range(depth):
        proj = _norm_matmul(xs, row3(attn_norm), w_in_p, l, tm=1024, tn=1280)
        mq, mk, mv = _mla_prep(proj, row3(mla_q_norm), row3(mla_kv_norm), wq_p, wkv_p, cos, sa, sb, l, tm=512)
        diff_extras = [
            (lam_p, pl.BlockSpec((None, 4, DIFF_HD), lambda b, h, i, l=l: (l, 0, 0))),
            (cst, pl.BlockSpec((None, 1, LANES), lambda b, h, i, l=l: (l, 0, 0))),
            (row3(diff_subln), pl.BlockSpec((None, 1, DIFF_VD), lambda b, h, i, l=l: (l, 0, 0))),
        ]
        o_diff = _attention(proj, proj, proj, diff_extras, batch=batch, seq=seq, heads=DIFF_HEADS,
                            qk_dim=LANES, q_col=COL_DQ // LANES, k_col=COL_DK // LANES,
                            v_col=COL_DV // LANES, bq=1024, bk=512, diff=True, name="diff_attn")
        o_gla = _gla(proj, gw_p, row3(gla_gate_b), row3(gla_norm), l, batch=batch, seq=seq, chunk=256)
        o_mla = _attention(mq, mk, mv, [], batch=batch, seq=seq, heads=MLA_HEADS,
                           qk_dim=MLA_QK, q_col=0, k_col=0, v_col=0, bq=2048, bk=512, diff=False,
                           name="mla_attn")
        x1 = _out_proj(xs, o_diff, o_gla, o_mla, wo_a, wo_b, wo_c, l, tm=512)
        xs = _ffn(x1, row3(ffn_norm), wg_p, wu_p, wd_p, final_norm.astype(F32)[None, :], l,
                  tm=512, tf=512, final_norm=(l == depth - 1))
    return xs.reshape(batch, seq, d)
```

```python
import functools
import math

import jax
import jax.numpy as jnp
from jax import lax
from jax.experimental import pallas as pl
from jax.experimental.pallas import tpu as pltpu

F32 = jnp.float32
BF16 = jnp.bfloat16

D_MODEL = 2048
DIFF_HEADS = 6
DIFF_HD = 64
DIFF_VD = 128
GLA_HEADS = 4
GLA_DK = 64
GLA_DV = 128
GLA_GATE_RANK = 16
GLA_TAU = 16.0
MLA_HEADS = 6
MLA_Q_RANK = 512
MLA_KV_RANK = 256
MLA_NOPE = 128
MLA_ROPE = 64
MLA_VD = 128
ROPE_THETA = 10000.0
D_FF = 5632
EPS = 1e-6
LOG2E = 1.4426950408889634

LANES = 128
VMEM_LIMIT = 56 * 1024 * 1024

PROJ_COLS = 5120
COL_DQ, COL_DK, COL_DV = 0, 768, 1536
COL_GQ, COL_GK, COL_MKVA, COL_GV, COL_GO, COL_MQA, COL_MKR, COL_GR = (
    2304, 2560, 2816, 3072, 3584, 4096, 4608, 4672)
MLA_QK = 256
STRIP = 256
ONES_ROWS = 16
GAP_LIMIT = 64.0


def _cparams(sem, flags=None):
    return pltpu.CompilerParams(dimension_semantics=sem, vmem_limit_bytes=VMEM_LIMIT, flags=flags)


def _rms(x, g):
    return x * lax.rsqrt(jnp.mean(x * x, axis=-1, keepdims=True) + EPS) * g


def _norm_matmul_kernel(x_ref, g_ref, w_ref, o_ref, h_ref):
    @pl.when(pl.program_id(1) == 0)
    def _():
        h_ref[...] = _rms(x_ref[...].astype(F32), g_ref[...]).astype(BF16)

    o_ref[...] = jnp.dot(h_ref[...], w_ref[...], preferred_element_type=F32).astype(o_ref.dtype)


def _norm_matmul(x, g, w, layer, *, tm, tn):
    t, d = x.shape
    n = w.shape[-1]
    return pl.pallas_call(
        _norm_matmul_kernel,
        grid=(t // tm, n // tn),
        in_specs=[
            pl.BlockSpec((tm, d), lambda i, j: (i, 0)),
            pl.BlockSpec((None, 1, d), lambda i, j: (layer, 0, 0)),
            pl.BlockSpec((None, d, tn), lambda i, j: (layer, 0, j)),
        ],
        out_specs=pl.BlockSpec((tm, tn), lambda i, j: (i, j)),
        out_shape=jax.ShapeDtypeStruct((t, n), BF16),
        scratch_shapes=[pltpu.VMEM((tm, d), BF16)],
        compiler_params=_cparams(("parallel", "arbitrary")),
        name="in_proj",
    )(x, g, w)


def _rope_table_kernel(pos_ref, inv_ref, cos_ref, sa_ref, sb_ref):
    ang = pos_ref[...] * inv_ref[...]
    c = jnp.cos(ang)
    s = jnp.sin(ang)
    lane = lax.broadcasted_iota(jnp.int32, ang.shape, 1)
    cos_ref[...] = jnp.where(lane < MLA_ROPE, c, 0.0)
    sa_ref[...] = jnp.where(lane < MLA_ROPE // 2, -s, 0.0)
    sb_ref[...] = jnp.where(lane < MLA_ROPE // 2, 0.0, jnp.where(lane < MLA_ROPE, s, 0.0))


def _rope_tables(posb, inv, *, tm):
    t = posb.shape[0]
    spec = pl.BlockSpec((tm, LANES), lambda i: (i, 0))
    return pl.pallas_call(
        _rope_table_kernel,
        grid=(t // tm,),
        in_specs=[spec, pl.BlockSpec((1, LANES), lambda i: (0, 0))],
        out_specs=[spec, spec, spec],
        out_shape=[jax.ShapeDtypeStruct((t, LANES), F32)] * 3,
        compiler_params=_cparams(("parallel",)),
        name="rope_tables",
    )(posb, inv)


def _rope128(x, cos, sa, sb):
    return x * cos + pltpu.roll(x, 96, 1) * sa + pltpu.roll(x, 32, 1) * sb


def _mla_prep_kernel(qa_ref, kva_ref, kr_ref, gq_ref, gkv_ref, wq_ref, wkv_ref,
                     cos_ref, sa_ref, sb_ref, q_ref, k_ref, v_ref):
    cos, sa, sb = cos_ref[...], sa_ref[...], sb_ref[...]
    scale = (MLA_NOPE + MLA_ROPE) ** -0.5 * LOG2E

    hq = _rms(qa_ref[...].astype(F32), gq_ref[...]).astype(BF16)
    q = jnp.dot(hq, wq_ref[...], preferred_element_type=F32)
    for h in range(MLA_HEADS):
        lo = h * MLA_QK
        q_ref[:, lo:lo + LANES] = (q[:, lo:lo + LANES] * scale).astype(BF16)
        rp = _rope128(q[:, lo + LANES:lo + 2 * LANES], cos, sa, sb)
        q_ref[:, lo + LANES:lo + 2 * LANES] = (rp * scale).astype(BF16)

    hk = _rms(kva_ref[...].astype(F32), gkv_ref[...]).astype(BF16)
    kv = jnp.dot(hk, wkv_ref[...], preferred_element_type=F32)
    lane = lax.broadcasted_iota(jnp.int32, cos.shape, 1)
    kr = jnp.where(lane < MLA_ROPE, kr_ref[...].astype(F32), 0.0)
    krr = _rope128(kr, cos, sa, sb).astype(BF16)
    for h in range(MLA_HEADS):
        lo = h * MLA_QK
        k_ref[:, lo:lo + LANES] = kv[:, h * LANES:(h + 1) * LANES].astype(BF16)
        k_ref[:, lo + LANES:lo + 2 * LANES] = krr
    v_ref[...] = kv[:, MLA_HEADS * LANES:].astype(BF16)


def _mla_prep(proj, gq, gkv, wq, wkv, cos, sa, sb, layer, *, tm):
    t = proj.shape[0]
    nq = MLA_HEADS * MLA_QK
    nv = MLA_HEADS * MLA_VD
    tab = pl.BlockSpec((tm, LANES), lambda i: (i, 0))
    return pl.pallas_call(
        _mla_prep_kernel,
        grid=(t // tm,),
        in_specs=[
            pl.BlockSpec((tm, MLA_Q_RANK), lambda i: (i, COL_MQA // MLA_Q_RANK)),
            pl.BlockSpec((tm, MLA_KV_RANK), lambda i: (i, COL_MKVA // MLA_KV_RANK)),
            pl.BlockSpec((tm, LANES), lambda i: (i, COL_MKR // LANES)),
            pl.BlockSpec((None, 1, MLA_Q_RANK), lambda i: (layer, 0, 0)),
            pl.BlockSpec((None, 1, MLA_KV_RANK), lambda i: (layer, 0, 0)),
            pl.BlockSpec((None, MLA_Q_RANK, nq), lambda i: (layer, 0, 0)),
            pl.BlockSpec((None, MLA_KV_RANK, nq), lambda i: (layer, 0, 0)),
            tab, tab, tab,
        ],
        out_specs=[
            pl.BlockSpec((tm, nq), lambda i: (i, 0)),
            pl.BlockSpec((tm, nq), lambda i: (i, 0)),
            pl.BlockSpec((tm, nv), lambda i: (i, 0)),
        ],
        out_shape=[
            jax.ShapeDtypeStruct((t, nq), BF16),
            jax.ShapeDtypeStruct((t, nq), BF16),
            jax.ShapeDtypeStruct((t, nv), BF16),
        ],
        compiler_params=_cparams(("parallel",)),
        name="mla_prep",
    )(proj, proj, proj, gq, gkv, wq, wkv, cos, sa, sb)


def _attn_kernel(*refs, n_maps, bq, bk, diff, gap_limit):
    if diff:
        q_ref, k_ref, v_ref, lam_ref, cst_ref, subln_ref, o_ref = refs[:7]
    else:
        q_ref, k_ref, v_ref, o_ref = refs[:4]
    qt_s, vt_s, p0, p1, al0, al1, r_s, over_s, acc_s = refs[-9:]
    pbuf, abuf = (p0, p1), (al0, al1)
    qi = pl.program_id(2)
    vd = LANES

    @pl.when(qi == 0)
    def _():
        def tr(c, carry):
            off = pl.multiple_of(c * bk, bk)
            vt_s[0:vd, pl.ds(off, bk)] = v_ref[pl.ds(off, bk), :].astype(F32).T.astype(BF16)
            return carry
        lax.fori_loop(0, vt_s.shape[1] // bk, tr, 0)
        vt_s[vd:, :] = jnp.ones((vt_s.shape[0] - vd, vt_s.shape[1]), BF16)

    if diff:
        qf = q_ref[...].astype(F32) * (DIFF_HD ** -0.5 * LOG2E)
        qt = qf.T
        row = lax.broadcasted_iota(jnp.int32, qt.shape, 0)
        qt_s[:, :bq] = jnp.where(row < DIFF_HD, qt, 0.0).astype(BF16)
        qt_s[:, bq:] = jnp.where(row < DIFF_HD, 0.0, qt).astype(BF16)
    else:
        qt_s[...] = q_ref[...].astype(F32).T.astype(BF16)

    acc_s[...] = jnp.zeros(acc_s.shape, F32)

    nq = n_maps * bq
    strips = [slice(c * STRIP, (c + 1) * STRIP) for c in range(nq // STRIP)]
    ratio = bq // bk
    first_diag = ratio * qi
    n_blocks = first_diag + ratio

    def scores(j, masked, cs, rows=bk):
        off = pl.multiple_of(j * bk, bk)
        s = jnp.dot(k_ref[pl.ds(off, rows), :], qt_s[:, cs], preferred_element_type=F32)
        if masked:
            key = lax.broadcasted_iota(jnp.int32, s.shape, 0) + (j * bk - qi * bq)
            qry = jnp.bitwise_and(lax.broadcasted_iota(jnp.int32, s.shape, 1) + cs.start, bq - 1)
            s = jnp.where(key <= qry, s, -jnp.inf)
        return s

    def first_reference(cs):
        mb = jnp.max(scores(0, True, cs, rows=ONES_ROWS), axis=0, keepdims=True)
        r_s[:, cs] = mb
        abuf[0][:, cs] = jnp.ones_like(mb)
        abuf[1][:, cs] = jnp.ones_like(mb)
        over_s[:, cs] = jnp.zeros_like(mb)

    def probs(j, slot, masked, cs):
        s = scores(j, masked, cs)
        r_old = r_s[:, cs]
        pbuf[slot][:, cs] = jnp.exp2(s - r_old).astype(BF16)
        mb = jnp.max(s, axis=0, keepdims=True)
        r_new = jnp.maximum(r_old, mb)
        abuf[1 - slot][:, cs] = jnp.exp2(r_old - r_new)
        over_s[:, cs] = jnp.maximum(over_s[:, cs], mb - r_old)
        r_s[:, cs] = r_new

    def values(j, slot, cs):
        off = pl.multiple_of(j * bk, bk)
        upd = jnp.dot(vt_s[:, pl.ds(off, bk)], pbuf[slot][:, cs],
                      preferred_element_type=F32)
        acc_s[:, cs] = acc_s[:, cs] * abuf[slot][:, cs] + upd

    def pair(j):
        for cs in strips:
            values(j, 0, cs)
            probs(j + 1, 1, False, cs)
        for cs in strips:
            values(j + 1, 1, cs)
            probs(j + 2, 0, False, cs)

    def two_pairs(u, carry):
        pair(4 * u)
        pair(4 * u + 2)
        return carry

    def visibility(d, cs):
        lo = cs.start % bq
        if lo + STRIP <= d * bk:
            return "none"
        return "all" if lo >= (d + 1) * bk else "part"

    def diag_probs(d, cs):
        if visibility(d, cs) != "none":
            probs(first_diag + d, d % 2, visibility(d, cs) == "part", cs)

    def diag_values(d, cs):
        if visibility(d, cs) != "none":
            values(first_diag + d, d % 2, cs)

    for cs in strips:
        first_reference(cs)
    for cs in strips:
        probs(0, 0, True, cs)
    n_pairs = jnp.maximum(first_diag // 2 - 1, 0)
    lax.fori_loop(0, n_pairs // 2, two_pairs, 0)

    @pl.when(jnp.bitwise_and(n_pairs, 1) == 1)
    def _():
        pair(2 * (n_pairs - 1))

    def diagonal_blocks():
        for d in range(1, ratio):
            for cs in strips:
                diag_values(d - 1, cs)
                diag_probs(d, cs)
        for cs in strips:
            diag_values(ratio - 1, cs)

    @pl.when(qi >= 1)
    def _():
        for cs in strips:
            values(first_diag - 2, 0, cs)
            probs(first_diag - 1, 1, False, cs)
        for cs in strips:
            values(first_diag - 1, 1, cs)
            diag_probs(0, cs)
        diagonal_blocks()

    @pl.when(qi == 0)
    def _():
        diagonal_blocks()

    @pl.when(jnp.max(over_s[...]) > gap_limit)
    def _():
        r_s[...] = jnp.full(r_s.shape, -jnp.inf, F32)
        acc_s[...] = jnp.zeros(acc_s.shape, F32)

        def exact(j, carry):
            off = pl.multiple_of(j * bk, bk)
            for cs in strips:
                s = scores(j, True, cs)
                m_old = r_s[:, cs]
                m_new = jnp.maximum(m_old, jnp.max(s, axis=0, keepdims=True))
                p = jnp.exp2(s - m_new).astype(BF16)
                upd = jnp.dot(vt_s[:, pl.ds(off, bk)], p, preferred_element_type=F32)
                acc_s[:, cs] = acc_s[:, cs] * jnp.exp2(m_old - m_new) + upd
                r_s[:, cs] = m_new
            return carry

        lax.fori_loop(0, n_blocks, exact, 0)

    ot = acc_s[0:vd, :] * (1.0 / acc_s[vd:vd + 1, :])
    if diff:
        lp = lam_ref[...]
        lam_init = cst_ref[:, 0:1]
        lam = (jnp.exp(jnp.sum(lp[0:1] * lp[1:2], axis=-1, keepdims=True))
               - jnp.exp(jnp.sum(lp[2:3] * lp[3:4], axis=-1, keepdims=True)) + lam_init)
        d = (ot[:, :bq] - lam * ot[:, bq:]).T
        o_ref[...] = (_rms(d, subln_ref[...]) * (1.0 - lam_init)).astype(o_ref.dtype)
    else:
        o_ref[...] = ot.T.astype(o_ref.dtype)


def _attention(q, k, v, extras, *, batch, seq, heads, qk_dim, q_col, k_col, v_col, bq, bk, diff, name,
               gap_limit=GAP_LIMIT):
    n_maps = 2 if diff else 1
    nq = n_maps * bq
    nqb = seq // bq
    assert bq % (2 * bk) == 0 and bk % STRIP == 0 and seq % bq == 0
    kernel = functools.partial(_attn_kernel, n_maps=n_maps, bq=bq, bk=bk, diff=diff, gap_limit=gap_limit)
    in_specs = [
        pl.BlockSpec((bq, qk_dim), lambda b, h, i: (b * nqb + i, q_col + h)),
        pl.BlockSpec((seq, qk_dim), lambda b, h, i: (b, k_col + h)),
        pl.BlockSpec((seq, LANES), lambda b, h, i: (b, v_col + h)),
    ]
    in_specs += [spec for _, spec in extras]
    row = pltpu.VMEM((1, nq), F32)
    return pl.pallas_call(
        kernel,
        grid=(batch, heads, nqb),
        in_specs=in_specs,
        out_specs=pl.BlockSpec((bq, LANES), lambda b, h, i: (b * nqb + i, h)),
        out_shape=jax.ShapeDtypeStruct((batch * seq, heads * LANES), BF16),
        scratch_shapes=[
            pltpu.VMEM((qk_dim, nq), BF16),
            pltpu.VMEM((LANES + ONES_ROWS, seq), BF16),
            pltpu.VMEM((bk, nq), BF16), pltpu.VMEM((bk, nq), BF16),
            row, row, row, row,
            pltpu.VMEM((LANES + ONES_ROWS, nq), F32),
        ],
        compiler_params=_cparams(("parallel", "parallel", "arbitrary")),
        name=name,
    )(q, k, v, *[a for a, _ in extras])


def _gla_kernel(gq_ref, gk_ref, gv_ref, go_ref, gr_ref, gw_ref, gb_ref, gn_ref, o_ref, st_ref, *, chunk):
    c = chunk

    @pl.when(pl.program_id(1) == 0)
    def _():
        st_ref[...] = jnp.zeros(st_ref.shape, F32)

    logits = jnp.dot(gr_ref[...], gw_ref[...], preferred_element_type=F32) + gb_ref[...]
    la = (jnp.minimum(logits, 0.0) - jnp.log(1.0 + jnp.exp(-jnp.abs(logits)))) * (1.0 / GLA_TAU)

    row = lax.broadcasted_iota(jnp.int32, la.shape, 0)
    b = la
    s = 1
    while s < c:
        b = b + jnp.where(row >= s, pltpu.roll(b, s, 0), 0.0)
        s *= 2

    q = gq_ref[...].astype(F32) * (GLA_DK ** -0.5)
    k = gk_ref[...].astype(F32)
    v = gv_ref[...]
    lane = lax.broadcasted_iota(jnp.int32, (c, LANES), 1)
    first_head = lane < GLA_DK

    ri = lax.broadcasted_iota(jnp.int32, (c, c), 0)
    ci = lax.broadcasted_iota(jnp.int32, (c, c), 1)
    lvl = jnp.where(ri > ci, jnp.bitwise_xor(ri, ci), 0)

    def head_scores(qt, kt):
        out = []
        for p in range(GLA_HEADS // 2):
            qp = qt[:, p * LANES:(p + 1) * LANES]
            kp = kt[:, p * LANES:(p + 1) * LANES].astype(BF16)
            for hh in range(2):
                qh = jnp.where(first_head if hh == 0 else jnp.logical_not(first_head), qp, 0.0)
                out.append(lax.dot_general(qh.astype(BF16), kp, (((1,), (1,)), ((), ())),
                                           preferred_element_type=F32))
        return out

    attn = [jnp.where(ri == ci, m, 0.0) for m in head_scores(q, k)]
    ref_b = b
    hs = 1
    bit = 0
    while hs < c:
        if hs > 1:
            ref_b = jnp.where(jnp.bitwise_and(row, hs - 1) < hs // 2, ref_b,
                              pltpu.roll(ref_b, hs // 2, 0))
        qt = q * jnp.exp(jnp.minimum(b - ref_b, 0.0))
        nxt = pltpu.roll(ref_b, c - hs, 0)
        kt = k * jnp.exp(jnp.minimum(nxt - b, 0.0))
        mask = lax.shift_right_logical(lvl, bit) == 1
        attn = [jnp.where(mask, m, a) for m, a in zip(head_scores(qt, kt), attn)]
        hs *= 2
        bit += 1

    b_end = b[c - 1:c, :]
    qb = (q * jnp.exp(b)).astype(BF16)
    kd = (k * jnp.exp(b_end - b)).astype(BF16)
    er = lax.broadcasted_iota(jnp.int32, (2 * GLA_DV, LANES), 0)
    ec = lax.broadcasted_iota(jnp.int32, (2 * GLA_DV, LANES), 1)
    bd_mask = (er < GLA_DV) == (ec < GLA_DK)

    outs = []
    for p in range(GLA_HEADS // 2):
        st = st_ref[p]
        inter = lax.dot_general(qb[:, p * LANES:(p + 1) * LANES], st.astype(BF16),
                                (((1,), (1,)), ((), ())), preferred_element_type=F32)
        vp = v[:, p * 2 * GLA_DV:(p + 1) * 2 * GLA_DV]
        for hh in range(2):
            h = 2 * p + hh
            intra = jnp.dot(attn[h].astype(BF16), vp[:, hh * GLA_DV:(hh + 1) * GLA_DV],
                            preferred_element_type=F32)
            outs.append(inter[:, hh * GLA_DV:(hh + 1) * GLA_DV] + intra)
        upd = lax.dot_general(vp, kd[:, p * LANES:(p + 1) * LANES], (((0,), (0,)), ((), ())),
                              preferred_element_type=F32)
        st_ref[p] = st * jnp.exp(b_end[:, p * LANES:(p + 1) * LANES]) + jnp.where(bd_mask, upd, 0.0)

    gn = gn_ref[...]
    for h in range(GLA_HEADS):
        g = go_ref[:, h * GLA_DV:(h + 1) * GLA_DV].astype(F32)
        y = _rms(outs[h], gn) * (g / (1.0 + jnp.exp(-g)))
        o_ref[:, h * GLA_DV:(h + 1) * GLA_DV] = y.astype(o_ref.dtype)


def _gla(proj, gw, gb, gn, layer, *, batch, seq, chunk):
    nc = seq // chunk
    qk = GLA_HEADS * GLA_DK
    vd = GLA_HEADS * GLA_DV
    return pl.pallas_call(
        functools.partial(_gla_kernel, chunk=chunk),
        grid=(batch, nc),
        in_specs=[
            pl.BlockSpec((chunk, qk), lambda b, c: (b * nc + c, COL_GQ // qk)),
            pl.BlockSpec((chunk, qk), lambda b, c: (b * nc + c, COL_GK // qk)),
            pl.BlockSpec((chunk, vd), lambda b, c: (b * nc + c, COL_GV // vd)),
            pl.BlockSpec((chunk, vd), lambda b, c: (b * nc + c, COL_GO // vd)),
            pl.BlockSpec((chunk, LANES), lambda b, c: (b * nc + c, COL_MKR // LANES)),
            pl.BlockSpec((None, LANES, qk), lambda b, c: (layer, 0, 0)),
            pl.BlockSpec((None, 1, qk), lambda b, c: (layer, 0, 0)),
            pl.BlockSpec((None, 1, GLA_DV), lambda b, c: (layer, 0, 0)),
        ],
        out_specs=pl.BlockSpec((chunk, vd), lambda b, c: (b * nc + c, 0)),
        out_shape=jax.ShapeDtypeStruct((batch * seq, vd), BF16),
        scratch_shapes=[pltpu.VMEM((GLA_HEADS // 2, 2 * GLA_DV, LANES), F32)],
        compiler_params=_cparams(("parallel", "arbitrary")),
        name="gla",
    )(proj, proj, proj, proj, proj, gw, gb, gn)


def _out_proj_kernel(x_ref, a_ref, b_ref, c_ref, wa_ref, wb_ref, wc_ref, o_ref):
    acc = jnp.dot(a_ref[...], wa_ref[...], preferred_element_type=F32)
    acc += jnp.dot(b_ref[...], wb_ref[...], preferred_element_type=F32)
    acc += jnp.dot(c_ref[...], wc_ref[...], preferred_element_type=F32)
    o_ref[...] = x_ref[...] + acc


def _out_proj(x, oa, ob, oc, wa, wb, wc, layer, *, tm):
    t, d = x.shape

    def act(a):
        return pl.BlockSpec((tm, a.shape[1]), lambda i: (i, 0))

    def wgt(w):
        return pl.BlockSpec((None, w.shape[1], d), lambda i: (layer, 0, 0))

    return pl.pallas_call(
        _out_proj_kernel,
        grid=(t // tm,),
        in_specs=[act(x), act(oa), act(ob), act(oc), wgt(wa), wgt(wb), wgt(wc)],
        out_specs=pl.BlockSpec((tm, d), lambda i: (i, 0)),
        out_shape=jax.ShapeDtypeStruct((t, d), F32),
        compiler_params=_cparams(("parallel",)),
        name="out_proj",
    )(x, oa, ob, oc, wa, wb, wc)


def _ffn_kernel(x_ref, g_ref, wg_ref, wu_ref, wd_ref, fg_ref, o_ref, h_ref, *, final_norm):
    f = pl.program_id(1)

    @pl.when(f == 0)
    def _():
        x = x_ref[...]
        h_ref[...] = _rms(x, g_ref[...]).astype(BF16)
        o_ref[...] = x

    h = h_ref[...]
    gate = jnp.dot(h, wg_ref[...], preferred_element_type=F32)
    up = jnp.dot(h, wu_ref[...], preferred_element_type=F32)
    act = (gate / (1.0 + jnp.exp(-gate)) * up).astype(BF16)
    o_ref[...] += jnp.dot(act, wd_ref[...], preferred_element_type=F32)

    if final_norm:
        @pl.when(f == pl.num_programs(1) - 1)
        def _():
            o_ref[...] = _rms(o_ref[...], fg_ref[...])


def _ffn(x, g, wg, wu, wd, fg, layer, *, tm, tf, final_norm):
    t, d = x.shape
    ff = wg.shape[-1]
    return pl.pallas_call(
        functools.partial(_ffn_kernel, final_norm=final_norm),
        grid=(t // tm, ff // tf),
        in_specs=[
            pl.BlockSpec((tm, d), lambda i, f: (i, 0)),
            pl.BlockSpec((None, 1, d), lambda i, f: (layer, 0, 0)),
            pl.BlockSpec((None, d, tf), lambda i, f: (layer, 0, f)),
            pl.BlockSpec((None, d, tf), lambda i, f: (layer, 0, f)),
            pl.BlockSpec((None, tf, d), lambda i, f: (layer, f, 0)),
            pl.BlockSpec((1, d), lambda i, f: (0, 0)),
        ],
        out_specs=pl.BlockSpec((tm, d), lambda i, f: (i, 0)),
        out_shape=jax.ShapeDtypeStruct((t, d), F32),
        scratch_shapes=[pltpu.VMEM((tm, d), BF16)],
        compiler_params=_cparams(("parallel", "arbitrary")),
        name="ffn",
    )(x, g, wg, wu, wd, fg)


def _prep_w_in(w_in):
    sizes = (768, 768, 768, 256, 256, 512, 16, 512, 512, 256, 64)
    offs = [0]
    for s in sizes:
        offs.append(offs[-1] + s)
    d_q, d_k, d_v, g_q, g_k, g_v, g_r, g_o, m_qa, m_kva, m_kr = (
        w_in[..., offs[i]:offs[i + 1]] for i in range(len(sizes)))
    used = COL_GR + GLA_GATE_RANK
    pad = jnp.zeros(w_in.shape[:-1] + (PROJ_COLS - used,), w_in.dtype)
    out = jnp.concatenate([d_q, d_k, d_v, g_q, g_k, m_kva, g_v, g_o, m_qa, m_kr, g_r, pad], axis=-1)
    return out.astype(BF16)


def _prep_w_q_b(w):
    depth, r, _ = w.shape
    w = w.reshape(depth, r, MLA_HEADS, MLA_NOPE + MLA_ROPE)
    w = jnp.pad(w, ((0, 0), (0, 0), (0, 0), (0, MLA_QK - MLA_NOPE - MLA_ROPE)))
    return w.reshape(depth, r, MLA_HEADS * MLA_QK).astype(BF16)


def _prep_w_kv_b(w):
    depth, r, _ = w.shape
    w = w.reshape(depth, r, MLA_HEADS, MLA_NOPE + MLA_VD)
    k = w[..., :MLA_NOPE].reshape(depth, r, MLA_HEADS * MLA_NOPE)
    v = w[..., MLA_NOPE:].reshape(depth, r, MLA_HEADS * MLA_VD)
    return jnp.concatenate([k, v], axis=-1).astype(BF16)


def _prep_gate_w(w):
    depth = w.shape[0]
    lo = COL_GR - COL_MKR
    out = jnp.zeros((depth, LANES, w.shape[-1]), w.dtype)
    return out.at[:, lo:lo + GLA_GATE_RANK, :].set(w).astype(BF16)


def kernel(x, positions, attn_norm, w_in, diff_lambda_q1, diff_lambda_k1, diff_lambda_q2, diff_lambda_k2, diff_subln, gla_gate_w, gla_gate_b, gla_norm, mla_q_norm, mla_w_q_b, mla_kv_norm, mla_w_kv_b, w_out, ffn_norm, w_gate, w_up, w_down, final_norm):
    batch, seq, d = x.shape
    depth = w_in.shape[0]
    t = batch * seq
    assert d == D_MODEL and seq % 2048 == 0

    w_in_p = _prep_w_in(w_in)
    wq_p = _prep_w_q_b(mla_w_q_b)
    wkv_p = _prep_w_kv_b(mla_w_kv_b)
    gw_p = _prep_gate_w(gla_gate_w)
    n_a, n_b = DIFF_HEADS * DIFF_VD, GLA_HEADS * GLA_DV
    wo_a = w_out[:, :n_a].astype(BF16)
    wo_b = w_out[:, n_a:n_a + n_b].astype(BF16)
    wo_c = w_out[:, n_a + n_b:].astype(BF16)
    wg_p, wu_p, wd_p = w_gate.astype(BF16), w_up.astype(BF16), w_down.astype(BF16)
    lam_p = jnp.stack([diff_lambda_q1, diff_lambda_k1, diff_lambda_q2, diff_lambda_k2], axis=1).astype(F32)
    lam_init = jnp.asarray([0.8 - 0.6 * math.exp(-0.3 * l) for l in range(depth)], F32)
    cst = jnp.zeros((depth, 1, LANES), F32).at[:, 0, 0].set(lam_init)

    def row3(a):
        return a.astype(F32)[:, None, :]

    posb = jnp.broadcast_to(positions.astype(F32).reshape(t, 1), (t, LANES))
    half = MLA_ROPE // 2
    inv = 1.0 / (ROPE_THETA ** (jnp.arange(0, MLA_ROPE, 2, dtype=F32) / MLA_ROPE))
    inv = jnp.concatenate([inv, inv, jnp.zeros((LANES - 2 * half,), F32)])[None, :]
    cos, sa, sb = _rope_tables(posb, inv, tm=1024)

    xs = x.reshape(t, d)
    for l in range(depth):
        proj = _norm_matmul(xs, row3(attn_norm), w_in_p, l, tm=1024, tn=1280)
        mq, mk, mv = _mla_prep(proj, row3(mla_q_norm), row3(mla_kv_norm), wq_p, wkv_p, cos, sa, sb, l, tm=512)
        diff_extras = [
            (lam_p, pl.BlockSpec((None, 4, DIFF_HD), lambda b, h, i, l=l: (l, 0, 0))),
            (cst, pl.BlockSpec((None, 1, LANES), lambda b, h, i, l=l: (l, 0, 0))),
            (row3(diff_subln), pl.BlockSpec((None, 1, DIFF_VD), lambda b, h, i, l=l: (l, 0, 0))),
        ]
        o_diff = _attention(proj, proj, proj, diff_extras, batch=batch, seq=seq, heads=DIFF_HEADS,
                            qk_dim=LANES, q_col=COL_DQ // LANES, k_col=COL_DK // LANES,
                            v_col=COL_DV // LANES, bq=1024, bk=512, diff=True, name="diff_attn")
        o_gla = _gla(proj, gw_p, row3(gla_gate_b), row3(gla_norm), l, batch=batch, seq=seq, chunk=256)
        o_mla = _attention(mq, mk, mv, [], batch=batch, seq=seq, heads=MLA_HEADS,
                           qk_dim=MLA_QK, q_col=0, k_col=0, v_col=0, bq=2048, bk=512, diff=False,
                           name="mla_attn")
        x1 = _out_proj(xs, o_diff, o_gla, o_mla, wo_a, wo_b, wo_c, l, tm=512)
        xs = _ffn(x1, row3(ffn_norm), wg_p, wu_p, wd_p, final_norm.astype(F32)[None, :], l,
                  tm=1024, tf=512, final_norm=(l == depth - 1))
    return xs.reshape(batch, seq, d)
```

```python
import functools
import math

import jax
import jax.numpy as jnp
from jax import lax
from jax.experimental import pallas as pl
from jax.experimental.pallas import tpu as pltpu

F32 = jnp.float32
BF16 = jnp.bfloat16

D_MODEL = 2048
DIFF_HEADS = 6
DIFF_HD = 64
DIFF_VD = 128
GLA_HEADS = 4
GLA_DK = 64
GLA_DV = 128
GLA_GATE_RANK = 16
GLA_TAU = 16.0
MLA_HEADS = 6
MLA_Q_RANK = 512
MLA_KV_RANK = 256
MLA_NOPE = 128
MLA_ROPE = 64
MLA_VD = 128
ROPE_THETA = 10000.0
D_FF = 5632
EPS = 1e-6
LOG2E = 1.4426950408889634

LANES = 128
VMEM_LIMIT = 56 * 1024 * 1024

PROJ_COLS = 4736
COL_DQ, COL_DK, COL_DV = 0, 768, 1536
COL_GQ, COL_GK, COL_MKVA, COL_GV, COL_GO, COL_MQA, COL_MKR, COL_GR = (
    2304, 2560, 2816, 3072, 3584, 4096, 4608, 4672)
MLA_QK = 256
STRIP = 256
ONES_ROWS = 16
GAP_LIMIT = 64.0


def _cparams(sem, flags=None):
    return pltpu.CompilerParams(dimension_semantics=sem, vmem_limit_bytes=VMEM_LIMIT, flags=flags)


def _rms(x, g):
    return x * lax.rsqrt(jnp.mean(x * x, axis=-1, keepdims=True) + EPS) * g


def _norm_matmul_kernel(x_ref, g_ref, w_ref, o_ref):
    h = _rms(x_ref[...].astype(F32), g_ref[...]).astype(BF16)
    o_ref[...] = jnp.dot(h, w_ref[...], preferred_element_type=F32).astype(o_ref.dtype)


def _norm_matmul(x, g, w, layer, *, tm):
    t, d = x.shape
    n = w.shape[-1]
    return pl.pallas_call(
        _norm_matmul_kernel,
        grid=(t // tm,),
        in_specs=[
            pl.BlockSpec((tm, d), lambda i: (i, 0)),
            pl.BlockSpec((None, 1, d), lambda i: (layer, 0, 0)),
            pl.BlockSpec((None, d, n), lambda i: (layer, 0, 0), pipeline_mode=pl.Buffered(1)),
        ],
        out_specs=pl.BlockSpec((tm, n), lambda i: (i, 0)),
        out_shape=jax.ShapeDtypeStruct((t, n), BF16),
        compiler_params=_cparams(("parallel",)),
        name="in_proj",
    )(x, g, w)


def _rope_table_kernel(pos_ref, inv_ref, cos_ref, sa_ref, sb_ref):
    ang = pos_ref[...] * inv_ref[...]
    c = jnp.cos(ang)
    s = jnp.sin(ang)
    lane = lax.broadcasted_iota(jnp.int32, ang.shape, 1)
    cos_ref[...] = jnp.where(lane < MLA_ROPE, c, 0.0)
    sa_ref[...] = jnp.where(lane < MLA_ROPE // 2, -s, 0.0)
    sb_ref[...] = jnp.where(lane < MLA_ROPE // 2, 0.0, jnp.where(lane < MLA_ROPE, s, 0.0))


def _rope_tables(posb, inv, *, tm):
    t = posb.shape[0]
    spec = pl.BlockSpec((tm, LANES), lambda i: (i, 0))
    return pl.pallas_call(
        _rope_table_kernel,
        grid=(t // tm,),
        in_specs=[spec, pl.BlockSpec((1, LANES), lambda i: (0, 0))],
        out_specs=[spec, spec, spec],
        out_shape=[jax.ShapeDtypeStruct((t, LANES), F32)] * 3,
        compiler_params=_cparams(("parallel",)),
        name="rope_tables",
    )(posb, inv)


def _rope128(x, cos, sa, sb):
    return x * cos + pltpu.roll(x, 96, 1) * sa + pltpu.roll(x, 32, 1) * sb


def _mla_prep_kernel(qa_ref, kva_ref, kr_ref, gq_ref, gkv_ref, wq_ref, wkv_ref,
                     cos_ref, sa_ref, sb_ref, q_ref, k_ref, v_ref):
    cos, sa, sb = cos_ref[...], sa_ref[...], sb_ref[...]
    scale = (MLA_NOPE + MLA_ROPE) ** -0.5 * LOG2E

    hq = _rms(qa_ref[...].astype(F32), gq_ref[...]).astype(BF16)
    q = jnp.dot(hq, wq_ref[...], preferred_element_type=F32)
    for h in range(MLA_HEADS):
        lo = h * MLA_QK
        q_ref[:, lo:lo + LANES] = (q[:, lo:lo + LANES] * scale).astype(BF16)
        rp = _rope128(q[:, lo + LANES:lo + 2 * LANES], cos, sa, sb)
        q_ref[:, lo + LANES:lo + 2 * LANES] = (rp * scale).astype(BF16)

    hk = _rms(kva_ref[...].astype(F32), gkv_ref[...]).astype(BF16)
    kv = jnp.dot(hk, wkv_ref[...], preferred_element_type=F32)
    lane = lax.broadcasted_iota(jnp.int32, cos.shape, 1)
    kr = jnp.where(lane < MLA_ROPE, kr_ref[...].astype(F32), 0.0)
    krr = _rope128(kr, cos, sa, sb).astype(BF16)
    for h in range(MLA_HEADS):
        lo = h * MLA_QK
        k_ref[:, lo:lo + LANES] = kv[:, h * LANES:(h + 1) * LANES].astype(BF16)
        k_ref[:, lo + LANES:lo + 2 * LANES] = krr
    v_ref[...] = kv[:, MLA_HEADS * LANES:].astype(BF16)


def _mla_prep(proj, gq, gkv, wq, wkv, cos, sa, sb, layer, *, tm):
    t = proj.shape[0]
    nq = MLA_HEADS * MLA_QK
    nv = MLA_HEADS * MLA_VD
    tab = pl.BlockSpec((tm, LANES), lambda i: (i, 0))
    return pl.pallas_call(
        _mla_prep_kernel,
        grid=(t // tm,),
        in_specs=[
            pl.BlockSpec((tm, MLA_Q_RANK), lambda i: (i, COL_MQA // MLA_Q_RANK)),
            pl.BlockSpec((tm, MLA_KV_RANK), lambda i: (i, COL_MKVA // MLA_KV_RANK)),
            pl.BlockSpec((tm, LANES), lambda i: (i, COL_MKR // LANES)),
            pl.BlockSpec((None, 1, MLA_Q_RANK), lambda i: (layer, 0, 0)),
            pl.BlockSpec((None, 1, MLA_KV_RANK), lambda i: (layer, 0, 0)),
            pl.BlockSpec((None, MLA_Q_RANK, nq), lambda i: (layer, 0, 0)),
            pl.BlockSpec((None, MLA_KV_RANK, nq), lambda i: (layer, 0, 0)),
            tab, tab, tab,
        ],
        out_specs=[
            pl.BlockSpec((tm, nq), lambda i: (i, 0)),
            pl.BlockSpec((tm, nq), lambda i: (i, 0)),
            pl.BlockSpec((tm, nv), lambda i: (i, 0)),
        ],
        out_shape=[
            jax.ShapeDtypeStruct((t, nq), BF16),
            jax.ShapeDtypeStruct((t, nq), BF16),
            jax.ShapeDtypeStruct((t, nv), BF16),
        ],
        compiler_params=_cparams(("parallel",)),
        name="mla_prep",
    )(proj, proj, proj, gq, gkv, wq, wkv, cos, sa, sb)


def _attn_kernel(*refs, n_maps, bq, bk, diff, gap_limit):
    if diff:
        q_ref, k_ref, v_ref, lam_ref, cst_ref, subln_ref, o_ref = refs[:7]
    else:
        q_ref, k_ref, v_ref, o_ref = refs[:4]
    qt_s, vt_s, p0, p1, al0, al1, r_s, over_s, acc_s = refs[-9:]
    pbuf, abuf = (p0, p1), (al0, al1)
    qi = pl.program_id(2)
    vd = LANES

    @pl.when(qi == 0)
    def _():
        def tr(c, carry):
            off = pl.multiple_of(c * bk, bk)
            vt_s[0:vd, pl.ds(off, bk)] = v_ref[pl.ds(off, bk), :].astype(F32).T.astype(BF16)
            return carry
        lax.fori_loop(0, vt_s.shape[1] // bk, tr, 0)
        vt_s[vd:, :] = jnp.ones((vt_s.shape[0] - vd, vt_s.shape[1]), BF16)

    if diff:
        qf = q_ref[...].astype(F32) * (DIFF_HD ** -0.5 * LOG2E)
        qt = qf.T
        row = lax.broadcasted_iota(jnp.int32, qt.shape, 0)
        qt_s[:, :bq] = jnp.where(row < DIFF_HD, qt, 0.0).astype(BF16)
        qt_s[:, bq:] = jnp.where(row < DIFF_HD, 0.0, qt).astype(BF16)
    else:
        qt_s[...] = q_ref[...].astype(F32).T.astype(BF16)

    acc_s[...] = jnp.zeros(acc_s.shape, F32)

    nq = n_maps * bq
    strips = [slice(c * STRIP, (c + 1) * STRIP) for c in range(nq // STRIP)]
    ratio = bq // bk
    first_diag = ratio * qi
    n_blocks = first_diag + ratio

    def scores(j, masked, cs, rows=bk):
        off = pl.multiple_of(j * bk, bk)
        s = jnp.dot(k_ref[pl.ds(off, rows), :], qt_s[:, cs], preferred_element_type=F32)
        if masked:
            key = lax.broadcasted_iota(jnp.int32, s.shape, 0) + (j * bk - qi * bq)
            qry = jnp.bitwise_and(lax.broadcasted_iota(jnp.int32, s.shape, 1) + cs.start, bq - 1)
            s = jnp.where(key <= qry, s, -jnp.inf)
        return s

    def first_reference(cs):
        mb = jnp.max(scores(0, True, cs, rows=ONES_ROWS), axis=0, keepdims=True)
        r_s[:, cs] = mb
        abuf[0][:, cs] = jnp.ones_like(mb)
        abuf[1][:, cs] = jnp.ones_like(mb)
        over_s[:, cs] = jnp.zeros_like(mb)

    def probs(j, slot, masked, cs):
        s = scores(j, masked, cs)
        r_old = r_s[:, cs]
        pbuf[slot][:, cs] = jnp.exp2(s - r_old).astype(BF16)
        mb = jnp.max(s, axis=0, keepdims=True)
        r_new = jnp.maximum(r_old, mb)
        abuf[1 - slot][:, cs] = jnp.exp2(r_old - r_new)
        over_s[:, cs] = jnp.maximum(over_s[:, cs], mb - r_old)
        r_s[:, cs] = r_new

    def values(j, slot, cs):
        off = pl.multiple_of(j * bk, bk)
        upd = jnp.dot(vt_s[:, pl.ds(off, bk)], pbuf[slot][:, cs],
                      preferred_element_type=F32)
        acc_s[:, cs] = acc_s[:, cs] * abuf[slot][:, cs] + upd

    def pair(j):
        for cs in strips:
            values(j, 0, cs)
            probs(j + 1, 1, False, cs)
        for cs in strips:
            values(j + 1, 1, cs)
            probs(j + 2, 0, False, cs)

    def two_pairs(u, carry):
        pair(4 * u)
        pair(4 * u + 2)
        return carry

    def visibility(d, cs):
        lo = cs.start % bq
        if lo + STRIP <= d * bk:
            return "none"
        return "all" if lo >= (d + 1) * bk else "part"

    def diag_probs(d, cs):
        if visibility(d, cs) != "none":
            probs(first_diag + d, d % 2, visibility(d, cs) == "part", cs)

    def diag_values(d, cs):
        if visibility(d, cs) != "none":
            values(first_diag + d, d % 2, cs)

    for cs in strips:
        first_reference(cs)
    @pl.when(qi == 0)
    def _():
        for cs in strips:
            probs(0, 0, True, cs)

    @pl.when(qi >= 1)
    def _():
        for cs in strips:
            probs(0, 0, False, cs)
    n_pairs = jnp.maximum(first_diag // 2 - 1, 0)
    lax.fori_loop(0, n_pairs // 2, two_pairs, 0)

    @pl.when(jnp.bitwise_and(n_pairs, 1) == 1)
    def _():
        pair(2 * (n_pairs - 1))

    def diagonal_blocks():
        for d in range(1, ratio):
            for cs in strips:
                diag_values(d - 1, cs)
                diag_probs(d, cs)
        for cs in strips:
            diag_values(ratio - 1, cs)

    @pl.when(qi >= 1)
    def _():
        for cs in strips:
            values(first_diag - 2, 0, cs)
            probs(first_diag - 1, 1, False, cs)
        for cs in strips:
            values(first_diag - 1, 1, cs)
            diag_probs(0, cs)
        diagonal_blocks()

    @pl.when(qi == 0)
    def _():
        diagonal_blocks()

    @pl.when(jnp.max(over_s[...]) > gap_limit)
    def _():
        r_s[...] = jnp.full(r_s.shape, -jnp.inf, F32)
        acc_s[...] = jnp.zeros(acc_s.shape, F32)

        def exact(j, carry):
            off = pl.multiple_of(j * bk, bk)
            for cs in strips:
                s = scores(j, True, cs)
                m_old = r_s[:, cs]
                m_new = jnp.maximum(m_old, jnp.max(s, axis=0, keepdims=True))
                p = jnp.exp2(s - m_new).astype(BF16)
                upd = jnp.dot(vt_s[:, pl.ds(off, bk)], p, preferred_element_type=F32)
                acc_s[:, cs] = acc_s[:, cs] * jnp.exp2(m_old - m_new) + upd
                r_s[:, cs] = m_new
            return carry

        lax.fori_loop(0, n_blocks, exact, 0)

    ot = acc_s[0:vd, :] * (1.0 / acc_s[vd:vd + 1, :])
    if diff:
        lp = lam_ref[...]
        lam_init = cst_ref[:, 0:1]
        lam = (jnp.exp(jnp.sum(lp[0:1] * lp[1:2], axis=-1, keepdims=True))
               - jnp.exp(jnp.sum(lp[2:3] * lp[3:4], axis=-1, keepdims=True)) + lam_init)
        d = (ot[:, :bq] - lam * ot[:, bq:]).T
        o_ref[...] = (_rms(d, subln_ref[...]) * (1.0 - lam_init)).astype(o_ref.dtype)
    else:
        o_ref[...] = ot.T.astype(o_ref.dtype)


def _attention(q, k, v, extras, *, batch, seq, heads, qk_dim, q_col, k_col, v_col, bq, bk, diff, name,
               gap_limit=GAP_LIMIT):
    n_maps = 2 if diff else 1
    nq = n_maps * bq
    nqb = seq // bq
    assert bq % (2 * bk) == 0 and bk % STRIP == 0 and seq % bq == 0
    kernel = functools.partial(_attn_kernel, n_maps=n_maps, bq=bq, bk=bk, diff=diff, gap_limit=gap_limit)
    in_specs = [
        pl.BlockSpec((bq, qk_dim), lambda b, h, i: (b * nqb + i, q_col + h)),
        pl.BlockSpec((seq, qk_dim), lambda b, h, i: (b, k_col + h)),
        pl.BlockSpec((seq, LANES), lambda b, h, i: (b, v_col + h)),
    ]
    in_specs += [spec for _, spec in extras]
    row = pltpu.VMEM((1, nq), F32)
    return pl.pallas_call(
        kernel,
        grid=(batch, heads, nqb),
        in_specs=in_specs,
        out_specs=pl.BlockSpec((bq, LANES), lambda b, h, i: (b * nqb + i, h)),
        out_shape=jax.ShapeDtypeStruct((batch * seq, heads * LANES), BF16),
        scratch_shapes=[
            pltpu.VMEM((qk_dim, nq), BF16),
            pltpu.VMEM((LANES + ONES_ROWS, seq), BF16),
            pltpu.VMEM((bk, nq), BF16), pltpu.VMEM((bk, nq), BF16),
            row, row, row, row,
            pltpu.VMEM((LANES + ONES_ROWS, nq), F32),
        ],
        compiler_params=_cparams(("parallel", "parallel", "arbitrary")),
        name=name,
    )(q, k, v, *[a for a, _ in extras])


def _gla_kernel(gq_ref, gk_ref, gv_ref, go_ref, gr_ref, gw_ref, gb_ref, gn_ref, o_ref, st_ref, *,
                block, chunk):
    c = chunk

    @pl.when(pl.program_id(1) == 0)
    def _():
        st_ref[...] = jnp.zeros(st_ref.shape, F32)

    logits = jnp.dot(gr_ref[...], gw_ref[...], preferred_element_type=F32) + gb_ref[...]
    soft = jnp.log2(1.0 + jnp.exp2(jnp.abs(logits) * (-LOG2E)))
    la = (jnp.minimum(logits, 0.0) * LOG2E - soft) * (1.0 / GLA_TAU)

    row = lax.broadcasted_iota(jnp.int32, la.shape, 0)
    in_chunk = jnp.bitwise_and(row, c - 1)
    b_all = la
    s = 1
    while s < c:
        b_all = b_all + jnp.where(in_chunk >= s, pltpu.roll(b_all, s, 0), 0.0)
        s *= 2

    lane = lax.broadcasted_iota(jnp.int32, (c, LANES), 1)
    first_head = lane < GLA_DK
    crow = lax.broadcasted_iota(jnp.int32, (c, GLA_HEADS * GLA_DK), 0)
    ri = lax.broadcasted_iota(jnp.int32, (c, c), 0)
    ci = lax.broadcasted_iota(jnp.int32, (c, c), 1)
    lvl = jnp.where(ri > ci, jnp.bitwise_xor(ri, ci), 0)
    diag = ri == ci
    n_levels = c.bit_length() - 1
    level_masks = [lax.shift_right_logical(lvl, bit) == 1 for bit in range(n_levels)]
    er = lax.broadcasted_iota(jnp.int32, (2 * GLA_DV, LANES), 0)
    ec = lax.broadcasted_iota(jnp.int32, (2 * GLA_DV, LANES), 1)
    bd_mask = (er < GLA_DV) == (ec < GLA_DK)
    gn = gn_ref[...]

    def head_scores(qt, kt):
        out = []
        for p in range(GLA_HEADS // 2):
            qp = qt[:, p * LANES:(p + 1) * LANES]
            kp = kt[:, p * LANES:(p + 1) * LANES].astype(BF16)
            for hh in range(2):
                qh = jnp.where(first_head if hh == 0 else jnp.logical_not(first_head), qp, 0.0)
                out.append(lax.dot_general(qh.astype(BF16), kp, (((1,), (1,)), ((), ())),
                                           preferred_element_type=F32))
        return out

    for sc in range(block // c):
        rows = slice(sc * c, (sc + 1) * c)
        b = b_all[rows]
        q = gq_ref[rows, :].astype(F32) * (GLA_DK ** -0.5)
        k = gk_ref[rows, :].astype(F32)
        v = gv_ref[rows, :]

        attn = [jnp.where(diag, m, 0.0) for m in head_scores(q, k)]
        ref_b = b
        for bit in range(n_levels):
            hs = 1 << bit
            if hs > 1:
                ref_b = jnp.where(jnp.bitwise_and(crow, hs - 1) < hs // 2, ref_b,
                                  pltpu.roll(ref_b, hs // 2, 0))
            qt = q * jnp.exp2(jnp.minimum(b - ref_b, 0.0))
            nxt = pltpu.roll(ref_b, c - hs, 0)
            kt = k * jnp.exp2(jnp.minimum(nxt - b, 0.0))
            attn = [jnp.where(level_masks[bit], m, a) for m, a in zip(head_scores(qt, kt), attn)]

        b_end = b[c - 1:c, :]
        qb = (q * jnp.exp2(b)).astype(BF16)
        kd = (k * jnp.exp2(b_end - b)).astype(BF16)

        for p in range(GLA_HEADS // 2):
            st = st_ref[p]
            inter = lax.dot_general(qb[:, p * LANES:(p + 1) * LANES], st.astype(BF16),
                                    (((1,), (1,)), ((), ())), preferred_element_type=F32)
            vp = v[:, p * 2 * GLA_DV:(p + 1) * 2 * GLA_DV]
            for hh in range(2):
                h = 2 * p + hh
                cols = slice(h * GLA_DV, (h + 1) * GLA_DV)
                intra = jnp.dot(attn[h].astype(BF16), vp[:, hh * GLA_DV:(hh + 1) * GLA_DV],
                                preferred_element_type=F32)
                out = inter[:, hh * GLA_DV:(hh + 1) * GLA_DV] + intra
                g = go_ref[rows, cols].astype(F32)
                o_ref[rows, cols] = (_rms(out, gn) * (g / (1.0 + jnp.exp(-g)))).astype(o_ref.dtype)
            upd = lax.dot_general(vp, kd[:, p * LANES:(p + 1) * LANES], (((0,), (0,)), ((), ())),
                                  preferred_element_type=F32)
            st_ref[p] = (st * jnp.exp2(b_end[:, p * LANES:(p + 1) * LANES])
                         + jnp.where(bd_mask, upd, 0.0))


def _gla(proj, gw, gb, gn, layer, *, batch, seq, block, chunk):
    nc = seq // block
    qk = GLA_HEADS * GLA_DK
    vd = GLA_HEADS * GLA_DV
    assert block % chunk == 0 and chunk & (chunk - 1) == 0
    return pl.pallas_call(
        functools.partial(_gla_kernel, block=block, chunk=chunk),
        grid=(batch, nc),
        in_specs=[
            pl.BlockSpec((block, qk), lambda b, c: (b * nc + c, COL_GQ // qk)),
            pl.BlockSpec((block, qk), lambda b, c: (b * nc + c, COL_GK // qk)),
            pl.BlockSpec((block, vd), lambda b, c: (b * nc + c, COL_GV // vd)),
            pl.BlockSpec((block, vd), lambda b, c: (b * nc + c, COL_GO // vd)),
            pl.BlockSpec((block, LANES), lambda b, c: (b * nc + c, COL_MKR // LANES)),
            pl.BlockSpec((None, LANES, qk), lambda b, c: (layer, 0, 0)),
            pl.BlockSpec((None, 1, qk), lambda b, c: (layer, 0, 0)),
            pl.BlockSpec((None, 1, GLA_DV), lambda b, c: (layer, 0, 0)),
        ],
        out_specs=pl.BlockSpec((block, vd), lambda b, c: (b * nc + c, 0)),
        out_shape=jax.ShapeDtypeStruct((batch * seq, vd), BF16),
        scratch_shapes=[pltpu.VMEM((GLA_HEADS // 2, 2 * GLA_DV, LANES), F32)],
        compiler_params=_cparams(("parallel", "arbitrary")),
        name="gla",
    )(proj, proj, proj, proj, proj, gw, gb, gn)


def _out_proj_kernel(x_ref, a_ref, b_ref, c_ref, wa_ref, wb_ref, wc_ref, o_ref):
    acc = jnp.dot(a_ref[...], wa_ref[...], preferred_element_type=F32)
    acc += jnp.dot(b_ref[...], wb_ref[...], preferred_element_type=F32)
    acc += jnp.dot(c_ref[...], wc_ref[...], preferred_element_type=F32)
    o_ref[...] = x_ref[...] + acc


def _out_proj(x, oa, ob, oc, wa, wb, wc, layer, *, tm):
    t, d = x.shape

    def act(a):
        return pl.BlockSpec((tm, a.shape[1]), lambda i: (i, 0))

    def wgt(w):
        return pl.BlockSpec((None, w.shape[1], d), lambda i: (layer, 0, 0))

    return pl.pallas_call(
        _out_proj_kernel,
        grid=(t // tm,),
        in_specs=[act(x), act(oa), act(ob), act(oc), wgt(wa), wgt(wb), wgt(wc)],
        out_specs=pl.BlockSpec((tm, d), lambda i: (i, 0)),
        out_shape=jax.ShapeDtypeStruct((t, d), F32),
        compiler_params=_cparams(("parallel",)),
        name="out_proj",
    )(x, oa, ob, oc, wa, wb, wc)


def _ffn_kernel(x_ref, g_ref, wg_ref, wu_ref, wd_ref, fg_ref, o_ref, h_ref, *, final_norm):
    f = pl.program_id(1)

    @pl.when(f == 0)
    def _():
        x = x_ref[...]
        h_ref[...] = _rms(x, g_ref[...]).astype(BF16)
        o_ref[...] = x

    h = h_ref[...]
    gate = jnp.dot(h, wg_ref[...], preferred_element_type=F32)
    up = jnp.dot(h, wu_ref[...], preferred_element_type=F32)
    act = (gate / (1.0 + jnp.exp(-gate)) * up).astype(BF16)
    o_ref[...] += jnp.dot(act, wd_ref[...], preferred_element_type=F32)

    if final_norm:
        @pl.when(f == pl.num_programs(1) - 1)
        def _():
            o_ref[...] = _rms(o_ref[...], fg_ref[...])


def _ffn(x, g, wg, wu, wd, fg, layer, *, tm, tf, final_norm):
    t, d = x.shape
    ff = wg.shape[-1]
    return pl.pallas_call(
        functools.partial(_ffn_kernel, final_norm=final_norm),
        grid=(t // tm, ff // tf),
        in_specs=[
            pl.BlockSpec((tm, d), lambda i, f: (i, 0)),
            pl.BlockSpec((None, 1, d), lambda i, f: (layer, 0, 0)),
            pl.BlockSpec((None, d, tf), lambda i, f: (layer, 0, f)),
            pl.BlockSpec((None, d, tf), lambda i, f: (layer, 0, f)),
            pl.BlockSpec((None, tf, d), lambda i, f: (layer, f, 0)),
            pl.BlockSpec((1, d), lambda i, f: (0, 0)),
        ],
        out_specs=pl.BlockSpec((tm, d), lambda i, f: (i, 0)),
        out_shape=jax.ShapeDtypeStruct((t, d), F32),
        scratch_shapes=[pltpu.VMEM((tm, d), BF16)],
        compiler_params=_cparams(("parallel", "arbitrary")),
        name="ffn",
    )(x, g, wg, wu, wd, fg)


def _prep_w_in(w_in):
    sizes = (768, 768, 768, 256, 256, 512, 16, 512, 512, 256, 64)
    offs = [0]
    for s in sizes:
        offs.append(offs[-1] + s)
    d_q, d_k, d_v, g_q, g_k, g_v, g_r, g_o, m_qa, m_kva, m_kr = (
        w_in[..., offs[i]:offs[i + 1]] for i in range(len(sizes)))
    used = COL_GR + GLA_GATE_RANK
    pad = jnp.zeros(w_in.shape[:-1] + (PROJ_COLS - used,), w_in.dtype)
    parts = [d_q, d_k, d_v, g_q, g_k, m_kva, g_v, g_o, m_qa, m_kr, g_r, pad]
    return jnp.concatenate([p.astype(BF16) for p in parts], axis=-1)


def _prep_w_q_b(w):
    depth, r, _ = w.shape
    w = w.reshape(depth, r, MLA_HEADS, MLA_NOPE + MLA_ROPE)
    w = jnp.pad(w, ((0, 0), (0, 0), (0, 0), (0, MLA_QK - MLA_NOPE - MLA_ROPE)))
    return w.reshape(depth, r, MLA_HEADS * MLA_QK).astype(BF16)


def _prep_w_kv_b(w):
    depth, r, _ = w.shape
    w = w.reshape(depth, r, MLA_HEADS, MLA_NOPE + MLA_VD)
    k = w[..., :MLA_NOPE].reshape(depth, r, MLA_HEADS * MLA_NOPE)
    v = w[..., MLA_NOPE:].reshape(depth, r, MLA_HEADS * MLA_VD)
    return jnp.concatenate([k, v], axis=-1).astype(BF16)


def _prep_gate_w(w):
    depth = w.shape[0]
    lo = COL_GR - COL_MKR
    out = jnp.zeros((depth, LANES, w.shape[-1]), w.dtype)
    return out.at[:, lo:lo + GLA_GATE_RANK, :].set(w).astype(BF16)


def kernel(x, positions, attn_norm, w_in, diff_lambda_q1, diff_lambda_k1, diff_lambda_q2, diff_lambda_k2, diff_subln, gla_gate_w, gla_gate_b, gla_norm, mla_q_norm, mla_w_q_b, mla_kv_norm, mla_w_kv_b, w_out, ffn_norm, w_gate, w_up, w_down, final_norm):
    batch, seq, d = x.shape
    depth = w_in.shape[0]
    t = batch * seq
    assert d == D_MODEL and seq % 2048 == 0

    w_in_p = _prep_w_in(w_in)
    wq_p = _prep_w_q_b(mla_w_q_b)
    wkv_p = _prep_w_kv_b(mla_w_kv_b)
    gw_p = _prep_gate_w(gla_gate_w)
    n_a, n_b = DIFF_HEADS * DIFF_VD, GLA_HEADS * GLA_DV
    wo_a = w_out[:, :n_a].astype(BF16)
    wo_b = w_out[:, n_a:n_a + n_b].astype(BF16)
    wo_c = w_out[:, n_a + n_b:].astype(BF16)
    wg_p, wu_p, wd_p = w_gate.astype(BF16), w_up.astype(BF16), w_down.astype(BF16)
    lam_p = jnp.stack([diff_lambda_q1, diff_lambda_k1, diff_lambda_q2, diff_lambda_k2], axis=1).astype(F32)
    lam_init = jnp.asarray([0.8 - 0.6 * math.exp(-0.3 * l) for l in range(depth)], F32)
    cst = jnp.zeros((depth, 1, LANES), F32).at[:, 0, 0].set(lam_init)

    def row3(a):
        return a.astype(F32)[:, None, :]

    posb = jnp.broadcast_to(positions.astype(F32).reshape(t, 1), (t, LANES))
    half = MLA_ROPE // 2
    inv = 1.0 / (ROPE_THETA ** (jnp.arange(0, MLA_ROPE, 2, dtype=F32) / MLA_ROPE))
    inv = jnp.concatenate([inv, inv, jnp.zeros((LANES - 2 * half,), F32)])[None, :]
    cos, sa, sb = _rope_tables(posb, inv, tm=1024)

    xs = x.reshape(t, d)
    for l in range(depth):
        proj = _norm_matmul(xs, row3(attn_norm), w_in_p, l, tm=512)
        mq, mk, mv = _mla_prep(proj, row3(mla_q_norm), row3(mla_kv_norm), wq_p, wkv_p, cos, sa, sb, l, tm=512)
        diff_extras = [
            (lam_p, pl.BlockSpec((None, 4, DIFF_HD), lambda b, h, i, l=l: (l, 0, 0))),
            (cst, pl.BlockSpec((None, 1, LANES), lambda b, h, i, l=l: (l, 0, 0))),
            (row3(diff_subln), pl.BlockSpec((None, 1, DIFF_VD), lambda b, h, i, l=l: (l, 0, 0))),
        ]
        o_diff = _attention(proj, proj, proj, diff_extras, batch=batch, seq=seq, heads=DIFF_HEADS,
                            qk_dim=LANES, q_col=COL_DQ // LANES, k_col=COL_DK // LANES,
                            v_col=COL_DV // LANES, bq=1024, bk=512, diff=True, name="diff_attn")
        o_gla = _gla(proj, gw_p, row3(gla_gate_b), row3(gla_norm), l, batch=batch, seq=seq,
                     block=256, chunk=128)
        o_mla = _attention(mq, mk, mv, [], batch=batch, seq=seq, heads=MLA_HEADS,
                           qk_dim=MLA_QK, q_col=0, k_col=0, v_col=0, bq=2048, bk=512, diff=False,
                           name="mla_attn")
        x1 = _out_proj(xs, o_diff, o_gla, o_mla, wo_a, wo_b, wo_c, l, tm=512)
        xs = _ffn(x1, row3(ffn_norm), wg_p, wu_p, wd_p, final_norm.astype(F32)[None, :], l,
                  tm=1024, tf=512, final_norm=(l == depth - 1))
    return xs.reshape(batch, seq, d)
```

```python
import functools
import math

import jax
import jax.numpy as jnp
from jax import lax
from jax.experimental import pallas as pl
from jax.experimental.pallas import tpu as pltpu

F32 = jnp.float32
BF16 = jnp.bfloat16

D_MODEL = 2048
DIFF_HEADS = 6
DIFF_HD = 64
DIFF_VD = 128
GLA_HEADS = 4
GLA_DK = 64
GLA_DV = 128
GLA_GATE_RANK = 16
GLA_TAU = 16.0
MLA_HEADS = 6
MLA_Q_RANK = 512
MLA_KV_RANK = 256
MLA_NOPE = 128
MLA_ROPE = 64
MLA_VD = 128
ROPE_THETA = 10000.0
D_FF = 5632
EPS = 1e-6
LOG2E = 1.4426950408889634

LANES = 128
VMEM_LIMIT = 56 * 1024 * 1024

PROJ_COLS = 4736
COL_DQ, COL_DK, COL_DV = 0, 768, 1536
COL_GQ, COL_GK, COL_MKVA, COL_GV, COL_GO, COL_MQA, COL_MKR, COL_GR = (
    2304, 2560, 2816, 3072, 3584, 4096, 4608, 4672)
MLA_QK = 256
STRIP = 256
ONES_ROWS = 16
GAP_LIMIT = 64.0

TILES = dict(
    rope_rows=1024,
    in_proj_rows=512,
    mla_prep_rows=512,
    diff_q=1024, diff_k=512,
    mla_q=2048, mla_k=512,
    gla_block=256, gla_chunk=128,
    out_proj_rows=512,
    ffn_rows=1024, ffn_cols=512,
)


def _cparams(sem, flags=None):
    return pltpu.CompilerParams(dimension_semantics=sem, vmem_limit_bytes=VMEM_LIMIT, flags=flags)


def _rms(x, g):
    return x * lax.rsqrt(jnp.mean(x * x, axis=-1, keepdims=True) + EPS) * g


def _norm_matmul_kernel(x_ref, g_ref, w_ref, o_ref):
    h = _rms(x_ref[...].astype(F32), g_ref[...]).astype(BF16)
    o_ref[...] = jnp.dot(h, w_ref[...], preferred_element_type=F32).astype(o_ref.dtype)


def _norm_matmul(x, g, w, layer, *, tm):
    t, d = x.shape
    n = w.shape[-1]
    return pl.pallas_call(
        _norm_matmul_kernel,
        grid=(t // tm,),
        in_specs=[
            pl.BlockSpec((tm, d), lambda i: (i, 0)),
            pl.BlockSpec((None, 1, d), lambda i: (layer, 0, 0)),
            pl.BlockSpec((None, d, n), lambda i: (layer, 0, 0), pipeline_mode=pl.Buffered(1)),
        ],
        out_specs=pl.BlockSpec((tm, n), lambda i: (i, 0)),
        out_shape=jax.ShapeDtypeStruct((t, n), BF16),
        compiler_params=_cparams(("parallel",)),
        name="in_proj",
    )(x, g, w)


def _rope_table_kernel(pos_ref, inv_ref, cos_ref, sa_ref, sb_ref):
    ang = pos_ref[...] * inv_ref[...]
    c = jnp.cos(ang)
    s = jnp.sin(ang)
    lane = lax.broadcasted_iota(jnp.int32, ang.shape, 1)
    cos_ref[...] = jnp.where(lane < MLA_ROPE, c, 0.0)
    sa_ref[...] = jnp.where(lane < MLA_ROPE // 2, -s, 0.0)
    sb_ref[...] = jnp.where(lane < MLA_ROPE // 2, 0.0, jnp.where(lane < MLA_ROPE, s, 0.0))


def _rope_tables(posb, inv, *, tm):
    t = posb.shape[0]
    spec = pl.BlockSpec((tm, LANES), lambda i: (i, 0))
    return pl.pallas_call(
        _rope_table_kernel,
        grid=(t // tm,),
        in_specs=[spec, pl.BlockSpec((1, LANES), lambda i: (0, 0))],
        out_specs=[spec, spec, spec],
        out_shape=[jax.ShapeDtypeStruct((t, LANES), F32)] * 3,
        compiler_params=_cparams(("parallel",)),
        name="rope_tables",
    )(posb, inv)


def _rope128(x, cos, sa, sb):
    return x * cos + pltpu.roll(x, 96, 1) * sa + pltpu.roll(x, 32, 1) * sb


def _mla_prep_kernel(qa_ref, kva_ref, kr_ref, gq_ref, gkv_ref, wq_ref, wkv_ref,
                     cos_ref, sa_ref, sb_ref, q_ref, k_ref, v_ref):
    cos, sa, sb = cos_ref[...], sa_ref[...], sb_ref[...]
    scale = (MLA_NOPE + MLA_ROPE) ** -0.5 * LOG2E

    hq = _rms(qa_ref[...].astype(F32), gq_ref[...]).astype(BF16)
    q = jnp.dot(hq, wq_ref[...], preferred_element_type=F32)
    for h in range(MLA_HEADS):
        lo = h * MLA_QK
        q_ref[:, lo:lo + LANES] = (q[:, lo:lo + LANES] * scale).astype(BF16)
        rp = _rope128(q[:, lo + LANES:lo + 2 * LANES], cos, sa, sb)
        q_ref[:, lo + LANES:lo + 2 * LANES] = (rp * scale).astype(BF16)

    hk = _rms(kva_ref[...].astype(F32), gkv_ref[...]).astype(BF16)
    kv = jnp.dot(hk, wkv_ref[...], preferred_element_type=F32)
    lane = lax.broadcasted_iota(jnp.int32, cos.shape, 1)
    kr = jnp.where(lane < MLA_ROPE, kr_ref[...].astype(F32), 0.0)
    krr = _rope128(kr, cos, sa, sb).astype(BF16)
    for h in range(MLA_HEADS):
        lo = h * MLA_QK
        k_ref[:, lo:lo + LANES] = kv[:, h * LANES:(h + 1) * LANES].astype(BF16)
        k_ref[:, lo + LANES:lo + 2 * LANES] = krr
    v_ref[...] = kv[:, MLA_HEADS * LANES:].astype(BF16)


def _mla_prep(proj, gq, gkv, wq, wkv, cos, sa, sb, layer, *, tm):
    t = proj.shape[0]
    nq = MLA_HEADS * MLA_QK
    nv = MLA_HEADS * MLA_VD
    tab = pl.BlockSpec((tm, LANES), lambda i: (i, 0))
    return pl.pallas_call(
        _mla_prep_kernel,
        grid=(t // tm,),
        in_specs=[
            pl.BlockSpec((tm, MLA_Q_RANK), lambda i: (i, COL_MQA // MLA_Q_RANK)),
            pl.BlockSpec((tm, MLA_KV_RANK), lambda i: (i, COL_MKVA // MLA_KV_RANK)),
            pl.BlockSpec((tm, LANES), lambda i: (i, COL_MKR // LANES)),
            pl.BlockSpec((None, 1, MLA_Q_RANK), lambda i: (layer, 0, 0)),
            pl.BlockSpec((None, 1, MLA_KV_RANK), lambda i: (layer, 0, 0)),
            pl.BlockSpec((None, MLA_Q_RANK, nq), lambda i: (layer, 0, 0)),
            pl.BlockSpec((None, MLA_KV_RANK, nq), lambda i: (layer, 0, 0)),
            tab, tab, tab,
        ],
        out_specs=[
            pl.BlockSpec((tm, nq), lambda i: (i, 0)),
            pl.BlockSpec((tm, nq), lambda i: (i, 0)),
            pl.BlockSpec((tm, nv), lambda i: (i, 0)),
        ],
        out_shape=[
            jax.ShapeDtypeStruct((t, nq), BF16),
            jax.ShapeDtypeStruct((t, nq), BF16),
            jax.ShapeDtypeStruct((t, nv), BF16),
        ],
        compiler_params=_cparams(("parallel",)),
        name="mla_prep",
    )(proj, proj, proj, gq, gkv, wq, wkv, cos, sa, sb)


def _attn_kernel(*refs, n_maps, bq, bk, diff, gap_limit):
    if diff:
        q_ref, k_ref, v_ref, lam_ref, cst_ref, subln_ref, o_ref = refs[:7]
    else:
        q_ref, k_ref, v_ref, o_ref = refs[:4]
    qt_s, vt_s, p0, p1, al0, al1, r_s, over_s, acc_s = refs[-9:]
    pbuf, abuf = (p0, p1), (al0, al1)
    qi = pl.program_id(2)
    vd = LANES

    @pl.when(qi == 0)
    def _():
        def tr(c, carry):
            off = pl.multiple_of(c * bk, bk)
            vt_s[0:vd, pl.ds(off, bk)] = v_ref[pl.ds(off, bk), :].astype(F32).T.astype(BF16)
            return carry
        lax.fori_loop(0, vt_s.shape[1] // bk, tr, 0)
        vt_s[vd:, :] = jnp.ones((vt_s.shape[0] - vd, vt_s.shape[1]), BF16)

    nq = n_maps * bq
    strips = [slice(c * STRIP, (c + 1) * STRIP) for c in range(nq // STRIP)]
    groups = [[strips[g + m * (bq // STRIP)] for m in range(n_maps)] for g in range(bq // STRIP)]
    ratio = bq // bk
    first_diag = ratio * qi
    n_blocks = first_diag + ratio

    def load_queries(group):
        rows = slice(group[0].start, group[0].stop)
        if diff:
            qt = (q_ref[rows, :].astype(F32) * (DIFF_HD ** -0.5 * LOG2E)).T
            row = lax.broadcasted_iota(jnp.int32, qt.shape, 0)
            qt_s[:, group[0]] = jnp.where(row < DIFF_HD, qt, 0.0).astype(BF16)
            qt_s[:, group[1]] = jnp.where(row < DIFF_HD, 0.0, qt).astype(BF16)
        else:
            qt_s[:, group[0]] = q_ref[rows, :].astype(F32).T.astype(BF16)
        for cs in group:
            acc_s[:, cs] = jnp.zeros((acc_s.shape[0], STRIP), F32)

    def store_output(group):
        rows = slice(group[0].start, group[0].stop)
        ot = [acc_s[0:vd, cs] * (1.0 / acc_s[vd:vd + 1, cs]) for cs in group]
        if diff:
            lp = lam_ref[...]
            lam_init = cst_ref[:, 0:1]
            lam = (jnp.exp(jnp.sum(lp[0:1] * lp[1:2], axis=-1, keepdims=True))
                   - jnp.exp(jnp.sum(lp[2:3] * lp[3:4], axis=-1, keepdims=True)) + lam_init)
            d = (ot[0] - lam * ot[1]).T
            o_ref[rows, :] = (_rms(d, subln_ref[...]) * (1.0 - lam_init)).astype(o_ref.dtype)
        else:
            o_ref[rows, :] = ot[0].T.astype(o_ref.dtype)

    def scores(j, masked, cs, rows=bk):
        off = pl.multiple_of(j * bk, bk)
        s = jnp.dot(k_ref[pl.ds(off, rows), :], qt_s[:, cs], preferred_element_type=F32)
        if masked:
            key = lax.broadcasted_iota(jnp.int32, s.shape, 0) + (j * bk - qi * bq)
            qry = jnp.bitwise_and(lax.broadcasted_iota(jnp.int32, s.shape, 1) + cs.start, bq - 1)
            s = jnp.where(key <= qry, s, -jnp.inf)
        return s

    def first_reference(cs):
        mb = jnp.max(scores(0, True, cs, rows=ONES_ROWS), axis=0, keepdims=True)
        r_s[:, cs] = mb
        abuf[0][:, cs] = jnp.ones_like(mb)
        abuf[1][:, cs] = jnp.ones_like(mb)
        over_s[:, cs] = jnp.zeros_like(mb)

    def probs(j, slot, masked, cs):
        s = scores(j, masked, cs)
        r_old = r_s[:, cs]
        pbuf[slot][:, cs] = jnp.exp2(s - r_old).astype(BF16)
        mb = jnp.max(s, axis=0, keepdims=True)
        r_new = jnp.maximum(r_old, mb)
        abuf[1 - slot][:, cs] = jnp.exp2(r_old - r_new)
        over_s[:, cs] = jnp.maximum(over_s[:, cs], mb - r_old)
        r_s[:, cs] = r_new

    def values(j, slot, cs):
        off = pl.multiple_of(j * bk, bk)
        upd = jnp.dot(vt_s[:, pl.ds(off, bk)], pbuf[slot][:, cs],
                      preferred_element_type=F32)
        acc_s[:, cs] = acc_s[:, cs] * abuf[slot][:, cs] + upd

    def pair(j):
        for cs in strips:
            values(j, 0, cs)
            probs(j + 1, 1, False, cs)
        for cs in strips:
            values(j + 1, 1, cs)
            probs(j + 2, 0, False, cs)

    def two_pairs(u, carry):
        pair(4 * u)
        pair(4 * u + 2)
        return carry

    def visibility(d, cs):
        lo = cs.start % bq
        if lo + STRIP <= d * bk:
            return "none"
        return "all" if lo >= (d + 1) * bk else "part"

    def diag_probs(d, cs):
        if visibility(d, cs) != "none":
            probs(first_diag + d, d % 2, visibility(d, cs) == "part", cs)

    def diag_values(d, cs):
        if visibility(d, cs) != "none":
            values(first_diag + d, d % 2, cs)

    def first_block(masked):
        for group in groups:
            load_queries(group)
            for cs in group:
                first_reference(cs)
                probs(0, 0, masked, cs)

    @pl.when(qi == 0)
    def _():
        first_block(True)

    @pl.when(qi >= 1)
    def _():
        first_block(False)

    n_pairs = jnp.maximum(first_diag // 2 - 1, 0)
    lax.fori_loop(0, n_pairs // 2, two_pairs, 0)

    @pl.when(jnp.bitwise_and(n_pairs, 1) == 1)
    def _():
        pair(2 * (n_pairs - 1))

    def diagonal_blocks():
        for d in range(1, ratio):
            for cs in strips:
                diag_values(d - 1, cs)
                diag_probs(d, cs)
        for cs in strips:
            diag_values(ratio - 1, cs)
        for group in groups:
            store_output(group)

    @pl.when(qi >= 1)
    def _():
        for cs in strips:
            values(first_diag - 2, 0, cs)
            probs(first_diag - 1, 1, False, cs)
        for cs in strips:
            values(first_diag - 1, 1, cs)
            diag_probs(0, cs)
        diagonal_blocks()

    @pl.when(qi == 0)
    def _():
        diagonal_blocks()

    @pl.when(jnp.max(over_s[...]) > gap_limit)
    def _():
        r_s[...] = jnp.full(r_s.shape, -jnp.inf, F32)
        acc_s[...] = jnp.zeros(acc_s.shape, F32)

        def exact(j, carry):
            off = pl.multiple_of(j * bk, bk)
            for cs in strips:
                s = scores(j, True, cs)
                m_old = r_s[:, cs]
                m_new = jnp.maximum(m_old, jnp.max(s, axis=0, keepdims=True))
                p = jnp.exp2(s - m_new).astype(BF16)
                upd = jnp.dot(vt_s[:, pl.ds(off, bk)], p, preferred_element_type=F32)
                acc_s[:, cs] = acc_s[:, cs] * jnp.exp2(m_old - m_new) + upd
                r_s[:, cs] = m_new
            return carry

        lax.fori_loop(0, n_blocks, exact, 0)
        for group in groups:
            store_output(group)


def _attention(q, k, v, extras, *, batch, seq, heads, qk_dim, q_col, k_col, v_col, bq, bk, diff, name,
               gap_limit=GAP_LIMIT):
    n_maps = 2 if diff else 1
    nq = n_maps * bq
    nqb = seq // bq
    assert bq % (2 * bk) == 0 and bk % STRIP == 0 and seq % bq == 0
    kernel = functools.partial(_attn_kernel, n_maps=n_maps, bq=bq, bk=bk, diff=diff, gap_limit=gap_limit)
    in_specs = [
        pl.BlockSpec((bq, qk_dim), lambda b, h, i: (b * nqb + i, q_col + h)),
        pl.BlockSpec((seq, qk_dim), lambda b, h, i: (b, k_col + h)),
        pl.BlockSpec((seq, LANES), lambda b, h, i: (b, v_col + h)),
    ]
    in_specs += [spec for _, spec in extras]
    row = pltpu.VMEM((1, nq), F32)
    return pl.pallas_call(
        kernel,
        grid=(batch, heads, nqb),
        in_specs=in_specs,
        out_specs=pl.BlockSpec((bq, LANES), lambda b, h, i: (b * nqb + i, h)),
        out_shape=jax.ShapeDtypeStruct((batch * seq, heads * LANES), BF16),
        scratch_shapes=[
            pltpu.VMEM((qk_dim, nq), BF16),
            pltpu.VMEM((LANES + ONES_ROWS, seq), BF16),
            pltpu.VMEM((bk, nq), BF16), pltpu.VMEM((bk, nq), BF16),
            row, row, row, row,
            pltpu.VMEM((LANES + ONES_ROWS, nq), F32),
        ],
        compiler_params=_cparams(("parallel", "parallel", "arbitrary")),
        name=name,
    )(q, k, v, *[a for a, _ in extras])


def _gla_kernel(gq_ref, gk_ref, gv_ref, go_ref, gr_ref, gw_ref, gb_ref, gn_ref, o_ref, st_ref, *,
                block, chunk):
    c = chunk

    @pl.when(pl.program_id(1) == 0)
    def _():
        st_ref[...] = jnp.zeros(st_ref.shape, F32)

    logits = jnp.dot(gr_ref[...], gw_ref[...], preferred_element_type=F32) + gb_ref[...]
    soft = jnp.log2(1.0 + jnp.exp2(jnp.abs(logits) * (-LOG2E)))
    la = (jnp.minimum(logits, 0.0) * LOG2E - soft) * (1.0 / GLA_TAU)

    row = lax.broadcasted_iota(jnp.int32, la.shape, 0)
    in_chunk = jnp.bitwise_and(row, c - 1)
    b_all = la
    s = 1
    while s < c:
        b_all = b_all + jnp.where(in_chunk >= s, pltpu.roll(b_all, s, 0), 0.0)
        s *= 2

    lane = lax.broadcasted_iota(jnp.int32, (c, LANES), 1)
    first_head = lane < GLA_DK
    crow = lax.broadcasted_iota(jnp.int32, (c, GLA_HEADS * GLA_DK), 0)
    ri = lax.broadcasted_iota(jnp.int32, (c, c), 0)
    ci = lax.broadcasted_iota(jnp.int32, (c, c), 1)
    lvl = jnp.where(ri > ci, jnp.bitwise_xor(ri, ci), 0)
    diag = ri == ci
    n_levels = c.bit_length() - 1
    level_masks = [lax.shift_right_logical(lvl, bit) == 1 for bit in range(n_levels)]
    er = lax.broadcasted_iota(jnp.int32, (2 * GLA_DV, LANES), 0)
    ec = lax.broadcasted_iota(jnp.int32, (2 * GLA_DV, LANES), 1)
    bd_mask = (er < GLA_DV) == (ec < GLA_DK)
    gn = gn_ref[...]

    def head_scores(qt, kt):
        out = []
        for p in range(GLA_HEADS // 2):
            qp = qt[:, p * LANES:(p + 1) * LANES]
            kp = kt[:, p * LANES:(p + 1) * LANES].astype(BF16)
            for hh in range(2):
                qh = jnp.where(first_head if hh == 0 else jnp.logical_not(first_head), qp, 0.0)
                out.append(lax.dot_general(qh.astype(BF16), kp, (((1,), (1,)), ((), ())),
                                           preferred_element_type=F32))
        return out

    for sc in range(block // c):
        rows = slice(sc * c, (sc + 1) * c)
        b = b_all[rows]
        q = gq_ref[rows, :].astype(F32) * (GLA_DK ** -0.5)
        k = gk_ref[rows, :].astype(F32)
        v = gv_ref[rows, :]

        attn = [jnp.where(diag, m, 0.0) for m in head_scores(q, k)]
        ref_b = b
        for bit in range(n_levels):
            hs = 1 << bit
            if hs > 1:
                ref_b = jnp.where(jnp.bitwise_and(crow, hs - 1) < hs // 2, ref_b,
                                  pltpu.roll(ref_b, hs // 2, 0))
            qt = q * jnp.exp2(jnp.minimum(b - ref_b, 0.0))
            nxt = pltpu.roll(ref_b, c - hs, 0)
            kt = k * jnp.exp2(jnp.minimum(nxt - b, 0.0))
            attn = [jnp.where(level_masks[bit], m, a) for m, a in zip(head_scores(qt, kt), attn)]

        b_end = b[c - 1:c, :]
        qb = (q * jnp.exp2(b)).astype(BF16)
        kd = (k * jnp.exp2(b_end - b)).astype(BF16)

        for p in range(GLA_HEADS // 2):
            st = st_ref[p]
            inter = lax.dot_general(qb[:, p * LANES:(p + 1) * LANES], st.astype(BF16),
                                    (((1,), (1,)), ((), ())), preferred_element_type=F32)
            vp = v[:, p * 2 * GLA_DV:(p + 1) * 2 * GLA_DV]
            for hh in range(2):
                h = 2 * p + hh
                cols = slice(h * GLA_DV, (h + 1) * GLA_DV)
                intra = jnp.dot(attn[h].astype(BF16), vp[:, hh * GLA_DV:(hh + 1) * GLA_DV],
                                preferred_element_type=F32)
                out = inter[:, hh * GLA_DV:(hh + 1) * GLA_DV] + intra
                g = go_ref[rows, cols].astype(F32)
                o_ref[rows, cols] = (_rms(out, gn) * (g / (1.0 + jnp.exp(-g)))).astype(o_ref.dtype)
            upd = lax.dot_general(vp, kd[:, p * LANES:(p + 1) * LANES], (((0,), (0,)), ((), ())),
                                  preferred_element_type=F32)
            st_ref[p] = (st * jnp.exp2(b_end[:, p * LANES:(p + 1) * LANES])
                         + jnp.where(bd_mask, upd, 0.0))


def _gla(proj, gw, gb, gn, layer, *, batch, seq, block, chunk):
    nc = seq // block
    qk = GLA_HEADS * GLA_DK
    vd = GLA_HEADS * GLA_DV
    assert block % chunk == 0 and chunk & (chunk - 1) == 0
    return pl.pallas_call(
        functools.partial(_gla_kernel, block=block, chunk=chunk),
        grid=(batch, nc),
        in_specs=[
            pl.BlockSpec((block, qk), lambda b, c: (b * nc + c, COL_GQ // qk)),
            pl.BlockSpec((block, qk), lambda b, c: (b * nc + c, COL_GK // qk)),
            pl.BlockSpec((block, vd), lambda b, c: (b * nc + c, COL_GV // vd)),
            pl.BlockSpec((block, vd), lambda b, c: (b * nc + c, COL_GO // vd)),
            pl.BlockSpec((block, LANES), lambda b, c: (b * nc + c, COL_MKR // LANES)),
            pl.BlockSpec((None, LANES, qk), lambda b, c: (layer, 0, 0)),
            pl.BlockSpec((None, 1, qk), lambda b, c: (layer, 0, 0)),
            pl.BlockSpec((None, 1, GLA_DV), lambda b, c: (layer, 0, 0)),
        ],
        out_specs=pl.BlockSpec((block, vd), lambda b, c: (b * nc + c, 0)),
        out_shape=jax.ShapeDtypeStruct((batch * seq, vd), BF16),
        scratch_shapes=[pltpu.VMEM((GLA_HEADS // 2, 2 * GLA_DV, LANES), F32)],
        compiler_params=_cparams(("parallel", "arbitrary")),
        name="gla",
    )(proj, proj, proj, proj, proj, gw, gb, gn)


def _out_proj_kernel(x_ref, a_ref, b_ref, c_ref, w_ref, o_ref):
    na, nb = a_ref.shape[1], b_ref.shape[1]
    acc = jnp.dot(a_ref[...], w_ref[0:na, :], preferred_element_type=F32)
    acc += jnp.dot(b_ref[...], w_ref[na:na + nb, :], preferred_element_type=F32)
    acc += jnp.dot(c_ref[...], w_ref[na + nb:, :], preferred_element_type=F32)
    o_ref[...] = x_ref[...] + acc


def _out_proj(x, oa, ob, oc, w, layer, *, tm):
    t, d = x.shape

    def act(a):
        return pl.BlockSpec((tm, a.shape[1]), lambda i: (i, 0))

    return pl.pallas_call(
        _out_proj_kernel,
        grid=(t // tm,),
        in_specs=[act(x), act(oa), act(ob), act(oc),
                  pl.BlockSpec((None, w.shape[1], d), lambda i: (layer, 0, 0))],
        out_specs=pl.BlockSpec((tm, d), lambda i: (i, 0)),
        out_shape=jax.ShapeDtypeStruct((t, d), F32),
        compiler_params=_cparams(("parallel",)),
        name="out_proj",
    )(x, oa, ob, oc, w)


def _ffn_kernel(x_ref, g_ref, wg_ref, wu_ref, wd_ref, fg_ref, o_ref, h_ref, *, final_norm):
    f = pl.program_id(1)

    @pl.when(f == 0)
    def _():
        x = x_ref[...]
        h_ref[...] = _rms(x, g_ref[...]).astype(BF16)
        o_ref[...] = x

    h = h_ref[...]
    gate = jnp.dot(h, wg_ref[...], preferred_element_type=F32)
    up = jnp.dot(h, wu_ref[...], preferred_element_type=F32)
    act = (gate / (1.0 + jnp.exp(-gate)) * up).astype(BF16)
    o_ref[...] += jnp.dot(act, wd_ref[...], preferred_element_type=F32)

    if final_norm:
        @pl.when(f == pl.num_programs(1) - 1)
        def _():
            o_ref[...] = _rms(o_ref[...], fg_ref[...])


def _ffn(x, g, wg, wu, wd, fg, layer, *, tm, tf, final_norm):
    t, d = x.shape
    ff = wg.shape[-1]
    return pl.pallas_call(
        functools.partial(_ffn_kernel, final_norm=final_norm),
        grid=(t // tm, ff // tf),
        in_specs=[
            pl.BlockSpec((tm, d), lambda i, f: (i, 0)),
            pl.BlockSpec((None, 1, d), lambda i, f: (layer, 0, 0)),
            pl.BlockSpec((None, d, tf), lambda i, f: (layer, 0, f)),
            pl.BlockSpec((None, d, tf), lambda i, f: (layer, 0, f)),
            pl.BlockSpec((None, tf, d), lambda i, f: (layer, f, 0)),
            pl.BlockSpec((1, d), lambda i, f: (0, 0)),
        ],
        out_specs=pl.BlockSpec((tm, d), lambda i, f: (i, 0)),
        out_shape=jax.ShapeDtypeStruct((t, d), F32),
        scratch_shapes=[pltpu.VMEM((tm, d), BF16)],
        compiler_params=_cparams(("parallel", "arbitrary")),
        name="ffn",
    )(x, g, wg, wu, wd, fg)


def _prep_w_in(w_in):
    sizes = (768, 768, 768, 256, 256, 512, 16, 512, 512, 256, 64)
    offs = [0]
    for s in sizes:
        offs.append(offs[-1] + s)
    d_q, d_k, d_v, g_q, g_k, g_v, g_r, g_o, m_qa, m_kva, m_kr = (
        w_in[..., offs[i]:offs[i + 1]] for i in range(len(sizes)))
    used = COL_GR + GLA_GATE_RANK
    pad = jnp.zeros(w_in.shape[:-1] + (PROJ_COLS - used,), w_in.dtype)
    parts = [d_q, d_k, d_v, g_q, g_k, m_kva, g_v, g_o, m_qa, m_kr, g_r, pad]
    return jnp.concatenate([p.astype(BF16) for p in parts], axis=-1)


def _prep_w_q_b(w):
    depth, r, _ = w.shape
    w = w.reshape(depth, r, MLA_HEADS, MLA_NOPE + MLA_ROPE)
    w = jnp.pad(w, ((0, 0), (0, 0), (0, 0), (0, MLA_QK - MLA_NOPE - MLA_ROPE)))
    return w.reshape(depth, r, MLA_HEADS * MLA_QK).astype(BF16)


def _prep_w_kv_b(w):
    depth, r, _ = w.shape
    w = w.reshape(depth, r, MLA_HEADS, MLA_NOPE + MLA_VD)
    k = w[..., :MLA_NOPE].reshape(depth, r, MLA_HEADS * MLA_NOPE)
    v = w[..., MLA_NOPE:].reshape(depth, r, MLA_HEADS * MLA_VD)
    return jnp.concatenate([k, v], axis=-1).astype(BF16)


def _prep_gate_w(w):
    depth = w.shape[0]
    lo = COL_GR - COL_MKR
    out = jnp.zeros((depth, LANES, w.shape[-1]), w.dtype)
    return out.at[:, lo:lo + GLA_GATE_RANK, :].set(w).astype(BF16)


def kernel(x, positions, attn_norm, w_in, diff_lambda_q1, diff_lambda_k1, diff_lambda_q2, diff_lambda_k2, diff_subln, gla_gate_w, gla_gate_b, gla_norm, mla_q_norm, mla_w_q_b, mla_kv_norm, mla_w_kv_b, w_out, ffn_norm, w_gate, w_up, w_down, final_norm):
    batch, seq, d = x.shape
    depth = w_in.shape[0]
    t = batch * seq
    assert d == D_MODEL and seq % max(TILES["diff_q"], TILES["mla_q"]) == 0

    w_in_p = _prep_w_in(w_in)
    wq_p = _prep_w_q_b(mla_w_q_b)
    wkv_p = _prep_w_kv_b(mla_w_kv_b)
    gw_p = _prep_gate_w(gla_gate_w)
    wo_p = w_out.astype(BF16)
    wg_p, wu_p, wd_p = w_gate.astype(BF16), w_up.astype(BF16), w_down.astype(BF16)
    lam_p = jnp.stack([diff_lambda_q1, diff_lambda_k1, diff_lambda_q2, diff_lambda_k2], axis=1).astype(F32)
    lam_init = jnp.asarray([0.8 - 0.6 * math.exp(-0.3 * l) for l in range(depth)], F32)
    cst = jnp.zeros((depth, 1, LANES), F32).at[:, 0, 0].set(lam_init)

    def row3(a):
        return a.astype(F32)[:, None, :]

    posb = jnp.broadcast_to(positions.astype(F32).reshape(t, 1), (t, LANES))
    half = MLA_ROPE // 2
    inv = 1.0 / (ROPE_THETA ** (jnp.arange(0, MLA_ROPE, 2, dtype=F32) / MLA_ROPE))
    inv = jnp.concatenate([inv, inv, jnp.zeros((LANES - 2 * half,), F32)])[None, :]
    cos, sa, sb = _rope_tables(posb, inv, tm=TILES["rope_rows"])

    xs = x.reshape(t, d)
    for l in range(depth):
        proj = _norm_matmul(xs, row3(attn_norm), w_in_p, l, tm=TILES["in_proj_rows"])
        mq, mk, mv = _mla_prep(proj, row3(mla_q_norm), row3(mla_kv_norm), wq_p, wkv_p, cos, sa, sb, l,
                               tm=TILES["mla_prep_rows"])
        diff_extras = [
            (lam_p, pl.BlockSpec((None, 4, DIFF_HD), lambda b, h, i, l=l: (l, 0, 0))),
            (cst, pl.BlockSpec((None, 1, LANES), lambda b, h, i, l=l: (l, 0, 0))),
            (row3(diff_subln), pl.BlockSpec((None, 1, DIFF_VD), lambda b, h, i, l=l: (l, 0, 0))),
        ]
        o_diff = _attention(proj, proj, proj, diff_extras, batch=batch, seq=seq, heads=DIFF_HEADS,
                            qk_dim=LANES, q_col=COL_DQ // LANES, k_col=COL_DK // LANES,
                            v_col=COL_DV // LANES, bq=TILES["diff_q"], bk=TILES["diff_k"], diff=True,
                            name="diff_attn")
        o_gla = _gla(proj, gw_p, row3(gla_gate_b), row3(gla_norm), l, batch=batch, seq=seq,
                     block=TILES["gla_block"], chunk=TILES["gla_chunk"])
        o_mla = _attention(mq, mk, mv, [], batch=batch, seq=seq, heads=MLA_HEADS,
                           qk_dim=MLA_QK, q_col=0, k_col=0, v_col=0, bq=TILES["mla_q"],
                           bk=TILES["mla_k"], diff=False,
                           name="mla_attn")
        x1 = _out_proj(xs, o_diff, o_gla, o_mla, wo_p, l, tm=TILES["out_proj_rows"])
        xs = _ffn(x1, row3(ffn_norm), wg_p, wu_p, wd_p, final_norm.astype(F32)[None, :], l,
                  tm=TILES["ffn_rows"], tf=TILES["ffn_cols"], final_norm=(l == depth - 1))
    return xs.reshape(batch, seq, d)
```

```python
import functools
import math

import jax
import jax.numpy as jnp
from jax import lax
from jax.experimental import pallas as pl
from jax.experimental.pallas import tpu as pltpu

F32 = jnp.float32
BF16 = jnp.bfloat16

D_MODEL = 2048
DIFF_HEADS = 6
DIFF_HD = 64
DIFF_VD = 128
GLA_HEADS = 4
GLA_DK = 64
GLA_DV = 128
GLA_GATE_RANK = 16
GLA_TAU = 16.0
MLA_HEADS = 6
MLA_Q_RANK = 512
MLA_KV_RANK = 256
MLA_NOPE = 128
MLA_ROPE = 64
MLA_VD = 128
ROPE_THETA = 10000.0
D_FF = 5632
EPS = 1e-6
LOG2E = 1.4426950408889634

LANES = 128
VMEM_LIMIT = 56 * 1024 * 1024

PROJ_COLS = 4736
COL_DQ, COL_DK, COL_DV = 0, 768, 1536
COL_GQ, COL_GK, COL_MKVA, COL_GV, COL_GO, COL_MQA, COL_MKR, COL_GR = (
    2304, 2560, 2816, 3072, 3584, 4096, 4608, 4672)
MLA_QK = 256
STRIP = 256
ONES_ROWS = 16
GAP_LIMIT = 64.0

TILES = dict(
    rope_rows=1024,
    in_proj_rows=512,
    mla_prep_rows=1024,
    diff_q=1024, diff_k=512,
    mla_q=2048, mla_k=512,
    gla_block=256, gla_chunk=128,
    out_proj_rows=1024,
    ffn_rows=1024, ffn_cols=512,
)


def _cparams(sem, flags=None):
    return pltpu.CompilerParams(dimension_semantics=sem, vmem_limit_bytes=VMEM_LIMIT, flags=flags)


def _rms(x, g):
    return x * lax.rsqrt(jnp.mean(x * x, axis=-1, keepdims=True) + EPS) * g


def _norm_matmul_kernel(x_ref, g_ref, w_ref, o_ref):
    h = _rms(x_ref[...].astype(F32), g_ref[...]).astype(BF16)
    o_ref[...] = jnp.dot(h, w_ref[...], preferred_element_type=F32).astype(o_ref.dtype)


def _norm_matmul(x, g, w, layer, *, tm):
    t, d = x.shape
    n = w.shape[-1]
    return pl.pallas_call(
        _norm_matmul_kernel,
        grid=(t // tm,),
        in_specs=[
            pl.BlockSpec((tm, d), lambda i: (i, 0)),
            pl.BlockSpec((None, 1, d), lambda i: (layer, 0, 0)),
            pl.BlockSpec((None, d, n), lambda i: (layer, 0, 0), pipeline_mode=pl.Buffered(1)),
        ],
        out_specs=pl.BlockSpec((tm, n), lambda i: (i, 0)),
        out_shape=jax.ShapeDtypeStruct((t, n), BF16),
        compiler_params=_cparams(("parallel",)),
        name="in_proj",
    )(x, g, w)


def _rope_table_kernel(pos_ref, inv_ref, cos_ref, sa_ref, sb_ref):
    ang = pos_ref[...] * inv_ref[...]
    c = jnp.cos(ang)
    s = jnp.sin(ang)
    lane = lax.broadcasted_iota(jnp.int32, ang.shape, 1)
    cos_ref[...] = jnp.where(lane < MLA_ROPE, c, 0.0)
    sa_ref[...] = jnp.where(lane < MLA_ROPE // 2, -s, 0.0)
    sb_ref[...] = jnp.where(lane < MLA_ROPE // 2, 0.0, jnp.where(lane < MLA_ROPE, s, 0.0))


def _rope_tables(posb, inv, *, tm):
    t = posb.shape[0]
    spec = pl.BlockSpec((tm, LANES), lambda i: (i, 0))
    return pl.pallas_call(
        _rope_table_kernel,
        grid=(t // tm,),
        in_specs=[spec, pl.BlockSpec((1, LANES), lambda i: (0, 0))],
        out_specs=[spec, spec, spec],
        out_shape=[jax.ShapeDtypeStruct((t, LANES), F32)] * 3,
        compiler_params=_cparams(("parallel",)),
        name="rope_tables",
    )(posb, inv)


def _rope128(x, cos, sa, sb):
    return x * cos + pltpu.roll(x, 96, 1) * sa + pltpu.roll(x, 32, 1) * sb


def _mla_prep_kernel(qa_ref, kva_ref, kr_ref, gq_ref, gkv_ref, wq_ref, wkv_ref,
                     cos_ref, sa_ref, sb_ref, q_ref, k_ref, v_ref):
    cos, sa, sb = cos_ref[...], sa_ref[...], sb_ref[...]
    scale = (MLA_NOPE + MLA_ROPE) ** -0.5 * LOG2E

    hq = _rms(qa_ref[...].astype(F32), gq_ref[...]).astype(BF16)
    q = jnp.dot(hq, wq_ref[...], preferred_element_type=F32)
    for h in range(MLA_HEADS):
        lo = h * MLA_QK
        q_ref[:, lo:lo + LANES] = (q[:, lo:lo + LANES] * scale).astype(BF16)
        rp = _rope128(q[:, lo + LANES:lo + 2 * LANES], cos, sa, sb)
        q_ref[:, lo + LANES:lo + 2 * LANES] = (rp * scale).astype(BF16)

    hk = _rms(kva_ref[...].astype(F32), gkv_ref[...]).astype(BF16)
    kv = jnp.dot(hk, wkv_ref[...], preferred_element_type=F32)
    lane = lax.broadcasted_iota(jnp.int32, cos.shape, 1)
    kr = jnp.where(lane < MLA_ROPE, kr_ref[...].astype(F32), 0.0)
    krr = _rope128(kr, cos, sa, sb).astype(BF16)
    for h in range(MLA_HEADS):
        lo = h * MLA_QK
        k_ref[:, lo:lo + LANES] = kv[:, h * LANES:(h + 1) * LANES].astype(BF16)
        k_ref[:, lo + LANES:lo + 2 * LANES] = krr
    v_ref[...] = kv[:, MLA_HEADS * LANES:].astype(BF16)


def _mla_prep(proj, gq, gkv, wq, wkv, cos, sa, sb, layer, *, tm):
    t = proj.shape[0]
    nq = MLA_HEADS * MLA_QK
    nv = MLA_HEADS * MLA_VD
    tab = pl.BlockSpec((tm, LANES), lambda i: (i, 0))
    return pl.pallas_call(
        _mla_prep_kernel,
        grid=(t // tm,),
        in_specs=[
            pl.BlockSpec((tm, MLA_Q_RANK), lambda i: (i, COL_MQA // MLA_Q_RANK)),
            pl.BlockSpec((tm, MLA_KV_RANK), lambda i: (i, COL_MKVA // MLA_KV_RANK)),
            pl.BlockSpec((tm, LANES), lambda i: (i, COL_MKR // LANES)),
            pl.BlockSpec((None, 1, MLA_Q_RANK), lambda i: (layer, 0, 0)),
            pl.BlockSpec((None, 1, MLA_KV_RANK), lambda i: (layer, 0, 0)),
            pl.BlockSpec((None, MLA_Q_RANK, nq), lambda i: (layer, 0, 0)),
            pl.BlockSpec((None, MLA_KV_RANK, nq), lambda i: (layer, 0, 0)),
            tab, tab, tab,
        ],
        out_specs=[
            pl.BlockSpec((tm, nq), lambda i: (i, 0)),
            pl.BlockSpec((tm, nq), lambda i: (i, 0)),
            pl.BlockSpec((tm, nv), lambda i: (i, 0)),
        ],
        out_shape=[
            jax.ShapeDtypeStruct((t, nq), BF16),
            jax.ShapeDtypeStruct((t, nq), BF16),
            jax.ShapeDtypeStruct((t, nv), BF16),
        ],
        compiler_params=_cparams(("parallel",)),
        name="mla_prep",
    )(proj, proj, proj, gq, gkv, wq, wkv, cos, sa, sb)


def _attn_kernel(*refs, n_maps, bq, bk, diff, gap_limit):
    if diff:
        q_ref, k_ref, v_ref, lam_ref, cst_ref, subln_ref, o_ref = refs[:7]
    else:
        q_ref, k_ref, v_ref, o_ref = refs[:4]
    qt_s, vt_s, p0, p1, al0, al1, r_s, over_s, acc_s = refs[-9:]
    pbuf, abuf = (p0, p1), (al0, al1)
    qi = pl.program_id(2)
    vd = LANES

    @pl.when(qi == 0)
    def _():
        def tr(c, carry):
            off = pl.multiple_of(c * bk, bk)
            vt_s[0:vd, pl.ds(off, bk)] = v_ref[pl.ds(off, bk), :].astype(F32).T.astype(BF16)
            return carry
        lax.fori_loop(0, vt_s.shape[1] // bk, tr, 0)
        vt_s[vd:, :] = jnp.ones((vt_s.shape[0] - vd, vt_s.shape[1]), BF16)

    nq = n_maps * bq
    strips = [slice(c * STRIP, (c + 1) * STRIP) for c in range(nq // STRIP)]
    groups = [[strips[g + m * (bq // STRIP)] for m in range(n_maps)] for g in range(bq // STRIP)]
    ratio = bq // bk
    first_diag = ratio * qi
    n_blocks = first_diag + ratio

    def load_queries(group):
        rows = slice(group[0].start, group[0].stop)
        if diff:
            qt = (q_ref[rows, :].astype(F32) * (DIFF_HD ** -0.5 * LOG2E)).T
            row = lax.broadcasted_iota(jnp.int32, qt.shape, 0)
            qt_s[:, group[0]] = jnp.where(row < DIFF_HD, qt, 0.0).astype(BF16)
            qt_s[:, group[1]] = jnp.where(row < DIFF_HD, 0.0, qt).astype(BF16)
        else:
            qt_s[:, group[0]] = q_ref[rows, :].astype(F32).T.astype(BF16)
        for cs in group:
            acc_s[:, cs] = jnp.zeros((acc_s.shape[0], STRIP), F32)

    def store_output(group):
        rows = slice(group[0].start, group[0].stop)
        ot = [acc_s[0:vd, cs] * (1.0 / acc_s[vd:vd + 1, cs]) for cs in group]
        if diff:
            lp = lam_ref[...]
            lam_init = cst_ref[:, 0:1]
            lam = (jnp.exp(jnp.sum(lp[0:1] * lp[1:2], axis=-1, keepdims=True))
                   - jnp.exp(jnp.sum(lp[2:3] * lp[3:4], axis=-1, keepdims=True)) + lam_init)
            d = (ot[0] - lam * ot[1]).T
            o_ref[rows, :] = (_rms(d, subln_ref[...]) * (1.0 - lam_init)).astype(o_ref.dtype)
        else:
            o_ref[rows, :] = ot[0].T.astype(o_ref.dtype)

    def scores(j, masked, cs, rows=bk):
        off = pl.multiple_of(j * bk, bk)
        s = jnp.dot(k_ref[pl.ds(off, rows), :], qt_s[:, cs], preferred_element_type=F32)
        if masked:
            key = lax.broadcasted_iota(jnp.int32, s.shape, 0) + (j * bk - qi * bq)
            qry = jnp.bitwise_and(lax.broadcasted_iota(jnp.int32, s.shape, 1) + cs.start, bq - 1)
            s = jnp.where(key <= qry, s, -jnp.inf)
        return s

    def first_reference(cs):
        mb = jnp.max(scores(0, True, cs, rows=ONES_ROWS), axis=0, keepdims=True)
        r_s[:, cs] = mb
        abuf[0][:, cs] = jnp.ones_like(mb)
        abuf[1][:, cs] = jnp.ones_like(mb)
        over_s[:, cs] = jnp.zeros_like(mb)

    def probs(j, slot, masked, cs):
        s = scores(j, masked, cs)
        r_old = r_s[:, cs]
        pbuf[slot][:, cs] = jnp.exp2(s - r_old).astype(BF16)
        mb = jnp.max(s, axis=0, keepdims=True)
        r_new = jnp.maximum(r_old, mb)
        abuf[1 - slot][:, cs] = jnp.exp2(r_old - r_new)
        over_s[:, cs] = jnp.maximum(over_s[:, cs], mb - r_old)
        r_s[:, cs] = r_new

    def values(j, slot, cs):
        off = pl.multiple_of(j * bk, bk)
        upd = jnp.dot(vt_s[:, pl.ds(off, bk)], pbuf[slot][:, cs],
                      preferred_element_type=F32)
        acc_s[:, cs] = acc_s[:, cs] * abuf[slot][:, cs] + upd

    def pair(j):
        for cs in strips:
            values(j, 0, cs)
            probs(j + 1, 1, False, cs)
        for cs in strips:
            values(j + 1, 1, cs)
            probs(j + 2, 0, False, cs)

    def two_pairs(u, carry):
        pair(4 * u)
        pair(4 * u + 2)
        return carry

    def visibility(d, cs):
        lo = cs.start % bq
        if lo + STRIP <= d * bk:
            return "none"
        return "all" if lo >= (d + 1) * bk else "part"

    def diag_probs(d, cs):
        if visibility(d, cs) != "none":
            probs(first_diag + d, d % 2, visibility(d, cs) == "part", cs)

    def diag_values(d, cs):
        if visibility(d, cs) != "none":
            values(first_diag + d, d % 2, cs)

    def first_block(masked):
        for group in groups:
            load_queries(group)
            for cs in group:
                first_reference(cs)
                probs(0, 0, masked, cs)

    @pl.when(qi == 0)
    def _():
        first_block(True)

    @pl.when(qi >= 1)
    def _():
        first_block(False)

    n_pairs = jnp.maximum(first_diag // 2 - 1, 0)
    lax.fori_loop(0, n_pairs // 2, two_pairs, 0)

    @pl.when(jnp.bitwise_and(n_pairs, 1) == 1)
    def _():
        pair(2 * (n_pairs - 1))

    def diagonal_blocks():
        for d in range(1, ratio):
            for cs in strips:
                diag_values(d - 1, cs)
                diag_probs(d, cs)
        for cs in strips:
            diag_values(ratio - 1, cs)
        for group in groups:
            store_output(group)

    @pl.when(qi >= 1)
    def _():
        for cs in strips:
            values(first_diag - 2, 0, cs)
            probs(first_diag - 1, 1, False, cs)
        for cs in strips:
            values(first_diag - 1, 1, cs)
            diag_probs(0, cs)
        diagonal_blocks()

    @pl.when(qi == 0)
    def _():
        diagonal_blocks()

    @pl.when(jnp.max(over_s[...]) > gap_limit)
    def _():
        r_s[...] = jnp.full(r_s.shape, -jnp.inf, F32)
        acc_s[...] = jnp.zeros(acc_s.shape, F32)

        def exact(j, carry):
            off = pl.multiple_of(j * bk, bk)
            for cs in strips:
                s = scores(j, True, cs)
                m_old = r_s[:, cs]
                m_new = jnp.maximum(m_old, jnp.max(s, axis=0, keepdims=True))
                p = jnp.exp2(s - m_new).astype(BF16)
                upd = jnp.dot(vt_s[:, pl.ds(off, bk)], p, preferred_element_type=F32)
                acc_s[:, cs] = acc_s[:, cs] * jnp.exp2(m_old - m_new) + upd
                r_s[:, cs] = m_new
            return carry

        lax.fori_loop(0, n_blocks, exact, 0)
        for group in groups:
            store_output(group)


def _attention(q, k, v, extras, *, batch, seq, heads, qk_dim, q_col, k_col, v_col, bq, bk, diff, name,
               gap_limit=GAP_LIMIT):
    n_maps = 2 if diff else 1
    nq = n_maps * bq
    nqb = seq // bq
    assert bq % (2 * bk) == 0 and bk % STRIP == 0 and seq % bq == 0
    kernel = functools.partial(_attn_kernel, n_maps=n_maps, bq=bq, bk=bk, diff=diff, gap_limit=gap_limit)
    in_specs = [
        pl.BlockSpec((bq, qk_dim), lambda b, h, i: (b * nqb + i, q_col + h)),
        pl.BlockSpec((seq, qk_dim), lambda b, h, i: (b, k_col + h)),
        pl.BlockSpec((seq, LANES), lambda b, h, i: (b, v_col + h)),
    ]
    in_specs += [spec for _, spec in extras]
    row = pltpu.VMEM((1, nq), F32)
    return pl.pallas_call(
        kernel,
        grid=(batch, heads, nqb),
        in_specs=in_specs,
        out_specs=pl.BlockSpec((bq, LANES), lambda b, h, i: (b * nqb + i, h)),
        out_shape=jax.ShapeDtypeStruct((batch * seq, heads * LANES), BF16),
        scratch_shapes=[
            pltpu.VMEM((qk_dim, nq), BF16),
            pltpu.VMEM((LANES + ONES_ROWS, seq), BF16),
            pltpu.VMEM((bk, nq), BF16), pltpu.VMEM((bk, nq), BF16),
            row, row, row, row,
            pltpu.VMEM((LANES + ONES_ROWS, nq), F32),
        ],
        compiler_params=_cparams(("parallel", "parallel", "arbitrary")),
        name=name,
    )(q, k, v, *[a for a, _ in extras])


def _gla_kernel(gq_ref, gk_ref, gv_ref, go_ref, gr_ref, gw_ref, gb_ref, gn_ref, o_ref, st_ref, *,
                block, chunk):
    c = chunk

    @pl.when(pl.program_id(1) == 0)
    def _():
        st_ref[...] = jnp.zeros(st_ref.shape, F32)

    logits = jnp.dot(gr_ref[...], gw_ref[...], preferred_element_type=F32) + gb_ref[...]
    soft = jnp.log2(1.0 + jnp.exp2(jnp.abs(logits) * (-LOG2E)))
    la = (jnp.minimum(logits, 0.0) * LOG2E - soft) * (1.0 / GLA_TAU)

    row = lax.broadcasted_iota(jnp.int32, la.shape, 0)
    in_chunk = jnp.bitwise_and(row, c - 1)
    b_all = la
    s = 1
    while s < c:
        b_all = b_all + jnp.where(in_chunk >= s, pltpu.roll(b_all, s, 0), 0.0)
        s *= 2

    lane = lax.broadcasted_iota(jnp.int32, (c, LANES), 1)
    first_head = lane < GLA_DK
    crow = lax.broadcasted_iota(jnp.int32, (c, GLA_HEADS * GLA_DK), 0)
    ri = lax.broadcasted_iota(jnp.int32, (c, c), 0)
    ci = lax.broadcasted_iota(jnp.int32, (c, c), 1)
    lvl = jnp.where(ri > ci, jnp.bitwise_xor(ri, ci), 0)
    diag = ri == ci
    n_levels = c.bit_length() - 1
    level_masks = [lax.shift_right_logical(lvl, bit) == 1 for bit in range(n_levels)]
    er = lax.broadcasted_iota(jnp.int32, (2 * GLA_DV, LANES), 0)
    ec = lax.broadcasted_iota(jnp.int32, (2 * GLA_DV, LANES), 1)
    bd_mask = (er < GLA_DV) == (ec < GLA_DK)
    gn = gn_ref[...]

    def head_scores(qt, kt):
        out = []
        for p in range(GLA_HEADS // 2):
            qp = qt[:, p * LANES:(p + 1) * LANES]
            kp = kt[:, p * LANES:(p + 1) * LANES].astype(BF16)
            for hh in range(2):
                qh = jnp.where(first_head if hh == 0 else jnp.logical_not(first_head), qp, 0.0)
                out.append(lax.dot_general(qh.astype(BF16), kp, (((1,), (1,)), ((), ())),
                                           preferred_element_type=F32))
        return out

    for sc in range(block // c):
        rows = slice(sc * c, (sc + 1) * c)
        b = b_all[rows]
        q = gq_ref[rows, :].astype(F32) * (GLA_DK ** -0.5)
        k = gk_ref[rows, :].astype(F32)
        v = gv_ref[rows, :]

        attn = [jnp.where(diag, m, 0.0) for m in head_scores(q, k)]
        ref_b = b
        for bit in range(n_levels):
            hs = 1 << bit
            if hs > 1:
                ref_b = jnp.where(jnp.bitwise_and(crow, hs - 1) < hs // 2, ref_b,
                                  pltpu.roll(ref_b, hs // 2, 0))
            qt = q * jnp.exp2(jnp.minimum(b - ref_b, 0.0))
            nxt = pltpu.roll(ref_b, c - hs, 0)
            kt = k * jnp.exp2(jnp.minimum(nxt - b, 0.0))
            attn = [jnp.where(level_masks[bit], m, a) for m, a in zip(head_scores(qt, kt), attn)]

        b_end = b[c - 1:c, :]
        qb = (q * jnp.exp2(b)).astype(BF16)
        kd = (k * jnp.exp2(b_end - b)).astype(BF16)

        for p in range(GLA_HEADS // 2):
            st = st_ref[p]
            inter = lax.dot_general(qb[:, p * LANES:(p + 1) * LANES], st.astype(BF16),
                                    (((1,), (1,)), ((), ())), preferred_element_type=F32)
            vp = v[:, p * 2 * GLA_DV:(p + 1) * 2 * GLA_DV]
            for hh in range(2):
                h = 2 * p + hh
                cols = slice(h * GLA_DV, (h + 1) * GLA_DV)
                intra = jnp.dot(attn[h].astype(BF16), vp[:, hh * GLA_DV:(hh + 1) * GLA_DV],
                                preferred_element_type=F32)
                out = inter[:, hh * GLA_DV:(hh + 1) * GLA_DV] + intra
                g = go_ref[rows, cols].astype(F32)
                o_ref[rows, cols] = (_rms(out, gn) * (g / (1.0 + jnp.exp(-g)))).astype(o_ref.dtype)
            upd = lax.dot_general(vp, kd[:, p * LANES:(p + 1) * LANES], (((0,), (0,)), ((), ())),
                                  preferred_element_type=F32)
            st_ref[p] = (st * jnp.exp2(b_end[:, p * LANES:(p + 1) * LANES])
                         + jnp.where(bd_mask, upd, 0.0))


def _gla(proj, gw, gb, gn, layer, *, batch, seq, block, chunk):
    nc = seq // block
    qk = GLA_HEADS * GLA_DK
    vd = GLA_HEADS * GLA_DV
    assert block % chunk == 0 and chunk & (chunk - 1) == 0
    return pl.pallas_call(
        functools.partial(_gla_kernel, block=block, chunk=chunk),
        grid=(batch, nc),
        in_specs=[
            pl.BlockSpec((block, qk), lambda b, c: (b * nc + c, COL_GQ // qk)),
            pl.BlockSpec((block, qk), lambda b, c: (b * nc + c, COL_GK // qk)),
            pl.BlockSpec((block, vd), lambda b, c: (b * nc + c, COL_GV // vd)),
            pl.BlockSpec((block, vd), lambda b, c: (b * nc + c, COL_GO // vd)),
            pl.BlockSpec((block, LANES), lambda b, c: (b * nc + c, COL_MKR // LANES)),
            pl.BlockSpec((None, LANES, qk), lambda b, c: (layer, 0, 0)),
            pl.BlockSpec((None, 1, qk), lambda b, c: (layer, 0, 0)),
            pl.BlockSpec((None, 1, GLA_DV), lambda b, c: (layer, 0, 0)),
        ],
        out_specs=pl.BlockSpec((block, vd), lambda b, c: (b * nc + c, 0)),
        out_shape=jax.ShapeDtypeStruct((batch * seq, vd), BF16),
        scratch_shapes=[pltpu.VMEM((GLA_HEADS // 2, 2 * GLA_DV, LANES), F32)],
        compiler_params=_cparams(("parallel", "arbitrary")),
        name="gla",
    )(proj, proj, proj, proj, proj, gw, gb, gn)


def _out_proj_kernel(x_ref, a_ref, b_ref, c_ref, w_ref, o_ref):
    na, nb = a_ref.shape[1], b_ref.shape[1]
    acc = jnp.dot(a_ref[...], w_ref[0:na, :], preferred_element_type=F32)
    acc += jnp.dot(b_ref[...], w_ref[na:na + nb, :], preferred_element_type=F32)
    acc += jnp.dot(c_ref[...], w_ref[na + nb:, :], preferred_element_type=F32)
    o_ref[...] = x_ref[...] + acc


def _out_proj(x, oa, ob, oc, w, layer, *, tm):
    t, d = x.shape

    def act(a):
        return pl.BlockSpec((tm, a.shape[1]), lambda i: (i, 0))

    return pl.pallas_call(
        _out_proj_kernel,
        grid=(t // tm,),
        in_specs=[act(x), act(oa), act(ob), act(oc),
                  pl.BlockSpec((None, w.shape[1], d), lambda i: (layer, 0, 0),
                               pipeline_mode=pl.Buffered(1))],
        out_specs=pl.BlockSpec((tm, d), lambda i: (i, 0)),
        out_shape=jax.ShapeDtypeStruct((t, d), F32),
        compiler_params=_cparams(("parallel",)),
        name="out_proj",
    )(x, oa, ob, oc, w)


def _ffn_kernel(x_ref, g_ref, wg_ref, wu_ref, wd_ref, fg_ref, o_ref, h_ref, *, final_norm):
    f = pl.program_id(1)

    @pl.when(f == 0)
    def _():
        x = x_ref[...]
        h_ref[...] = _rms(x, g_ref[...]).astype(BF16)
        o_ref[...] = x

    h = h_ref[...]
    gate = jnp.dot(h, wg_ref[...], preferred_element_type=F32)
    up = jnp.dot(h, wu_ref[...], preferred_element_type=F32)
    act = (gate / (1.0 + jnp.exp(-gate)) * up).astype(BF16)
    o_ref[...] += jnp.dot(act, wd_ref[...], preferred_element_type=F32)

    if final_norm:
        @pl.when(f == pl.num_programs(1) - 1)
        def _():
            o_ref[...] = _rms(o_ref[...], fg_ref[...])


def _ffn(x, g, wg, wu, wd, fg, layer, *, tm, tf, final_norm):
    t, d = x.shape
    ff = wg.shape[-1]
    return pl.pallas_call(
        functools.partial(_ffn_kernel, final_norm=final_norm),
        grid=(t // tm, ff // tf),
        in_specs=[
            pl.BlockSpec((tm, d), lambda i, f: (i, 0)),
            pl.BlockSpec((None, 1, d), lambda i, f: (layer, 0, 0)),
            pl.BlockSpec((None, d, tf), lambda i, f: (layer, 0, f)),
            pl.BlockSpec((None, d, tf), lambda i, f: (layer, 0, f)),
            pl.BlockSpec((None, tf, d), lambda i, f: (layer, f, 0)),
            pl.BlockSpec((1, d), lambda i, f: (0, 0)),
        ],
        out_specs=pl.BlockSpec((tm, d), lambda i, f: (i, 0)),
        out_shape=jax.ShapeDtypeStruct((t, d), F32),
        scratch_shapes=[pltpu.VMEM((tm, d), BF16)],
        compiler_params=_cparams(("parallel", "arbitrary")),
        name="ffn",
    )(x, g, wg, wu, wd, fg)


def _prep_w_in(w_in):
    sizes = (768, 768, 768, 256, 256, 512, 16, 512, 512, 256, 64)
    offs = [0]
    for s in sizes:
        offs.append(offs[-1] + s)
    d_q, d_k, d_v, g_q, g_k, g_v, g_r, g_o, m_qa, m_kva, m_kr = (
        w_in[..., offs[i]:offs[i + 1]] for i in range(len(sizes)))
    used = COL_GR + GLA_GATE_RANK
    pad = jnp.zeros(w_in.shape[:-1] + (PROJ_COLS - used,), w_in.dtype)
    parts = [d_q, d_k, d_v, g_q, g_k, m_kva, g_v, g_o, m_qa, m_kr, g_r, pad]
    return jnp.concatenate([p.astype(BF16) for p in parts], axis=-1)


def _prep_w_q_b(w):
    depth, r, _ = w.shape
    w = w.reshape(depth, r, MLA_HEADS, MLA_NOPE + MLA_ROPE)
    w = jnp.pad(w, ((0, 0), (0, 0), (0, 0), (0, MLA_QK - MLA_NOPE - MLA_ROPE)))
    return w.reshape(depth, r, MLA_HEADS * MLA_QK).astype(BF16)


def _prep_w_kv_b(w):
    depth, r, _ = w.shape
    w = w.reshape(depth, r, MLA_HEADS, MLA_NOPE + MLA_VD)
    k = w[..., :MLA_NOPE].reshape(depth, r, MLA_HEADS * MLA_NOPE)
    v = w[..., MLA_NOPE:].reshape(depth, r, MLA_HEADS * MLA_VD)
    return jnp.concatenate([k, v], axis=-1).astype(BF16)


def _prep_gate_w(w):
    depth = w.shape[0]
    lo = COL_GR - COL_MKR
    out = jnp.zeros((depth, LANES, w.shape[-1]), w.dtype)
    return out.at[:, lo:lo + GLA_GATE_RANK, :].set(w).astype(BF16)


def kernel(x, positions, attn_norm, w_in, diff_lambda_q1, diff_lambda_k1, diff_lambda_q2, diff_lambda_k2, diff_subln, gla_gate_w, gla_gate_b, gla_norm, mla_q_norm, mla_w_q_b, mla_kv_norm, mla_w_kv_b, w_out, ffn_norm, w_gate, w_up, w_down, final_norm):
    batch, seq, d = x.shape
    depth = w_in.shape[0]
    t = batch * seq
    assert d == D_MODEL and seq % max(TILES["diff_q"], TILES["mla_q"]) == 0

    w_in_p = _prep_w_in(w_in)
    wq_p = _prep_w_q_b(mla_w_q_b)
    wkv_p = _prep_w_kv_b(mla_w_kv_b)
    gw_p = _prep_gate_w(gla_gate_w)
    wo_p = w_out.astype(BF16)
    wg_p, wu_p, wd_p = w_gate.astype(BF16), w_up.astype(BF16), w_down.astype(BF16)
    lam_p = jnp.stack([diff_lambda_q1, diff_lambda_k1, diff_lambda_q2, diff_lambda_k2], axis=1).astype(F32)
    lam_init = jnp.asarray([0.8 - 0.6 * math.exp(-0.3 * l) for l in range(depth)], F32)
    cst = jnp.zeros((depth, 1, LANES), F32).at[:, 0, 0].set(lam_init)

    def row3(a):
        return a.astype(F32)[:, None, :]

    posb = jnp.broadcast_to(positions.astype(F32).reshape(t, 1), (t, LANES))
    half = MLA_ROPE // 2
    inv = 1.0 / (ROPE_THETA ** (jnp.arange(0, MLA_ROPE, 2, dtype=F32) / MLA_ROPE))
    inv = jnp.concatenate([inv, inv, jnp.zeros((LANES - 2 * half,), F32)])[None, :]
    cos, sa, sb = _rope_tables(posb, inv, tm=TILES["rope_rows"])

    xs = x.reshape(t, d)
    for l in range(depth):
        proj = _norm_matmul(xs, row3(attn_norm), w_in_p, l, tm=TILES["in_proj_rows"])
        mq, mk, mv = _mla_prep(proj, row3(mla_q_norm), row3(mla_kv_norm), wq_p, wkv_p, cos, sa, sb, l,
                               tm=TILES["mla_prep_rows"])
        diff_extras = [
            (lam_p, pl.BlockSpec((None, 4, DIFF_HD), lambda b, h, i, l=l: (l, 0, 0))),
            (cst, pl.BlockSpec((None, 1, LANES), lambda b, h, i, l=l: (l, 0, 0))),
            (row3(diff_subln), pl.BlockSpec((None, 1, DIFF_VD), lambda b, h, i, l=l: (l, 0, 0))),
        ]
        o_diff = _attention(proj, proj, proj, diff_extras, batch=batch, seq=seq, heads=DIFF_HEADS,
                            qk_dim=LANES, q_col=COL_DQ // LANES, k_col=COL_DK // LANES,
                            v_col=COL_DV // LANES, bq=TILES["diff_q"], bk=TILES["diff_k"], diff=True,
                            name="diff_attn")
        o_gla = _gla(proj, gw_p, row3(gla_gate_b), row3(gla_norm), l, batch=batch, seq=seq,
                     block=TILES["gla_block"], chunk=TILES["gla_chunk"])
        o_mla = _attention(mq, mk, mv, [], batch=batch, seq=seq, heads=MLA_HEADS,
                           qk_dim=MLA_QK, q_col=0, k_col=0, v_col=0, bq=TILES["mla_q"],
                           bk=TILES["mla_k"], diff=False,
                           name="mla_attn")
        x1 = _out_proj(xs, o_diff, o_gla, o_mla, wo_p, l, tm=TILES["out_proj_rows"])
        xs = _ffn(x1, row3(ffn_norm), wg_p, wu_p, wd_p, final_norm.astype(F32)[None, :], l,
                  tm=TILES["ffn_rows"], tf=TILES["ffn_cols"], final_norm=(l == depth - 1))
    return xs.reshape(batch, seq, d)
```

```python
import functools
import math

import jax
import jax.numpy as jnp
from jax import lax
from jax.experimental import pallas as pl
from jax.experimental.pallas import tpu as pltpu

F32 = jnp.float32
BF16 = jnp.bfloat16

D_MODEL = 2048
DIFF_HEADS = 6
DIFF_HD = 64
DIFF_VD = 128
GLA_HEADS = 4
GLA_DK = 64
GLA_DV = 128
GLA_GATE_RANK = 16
GLA_TAU = 16.0
MLA_HEADS = 6
MLA_Q_RANK = 512
MLA_KV_RANK = 256
MLA_NOPE = 128
MLA_ROPE = 64
MLA_VD = 128
ROPE_THETA = 10000.0
D_FF = 5632
EPS = 1e-6
LOG2E = 1.4426950408889634

LANES = 128
VMEM_LIMIT = 56 * 1024 * 1024

PROJ_COLS = 4736
COL_DQ, COL_DK, COL_DV = 0, 768, 1536
COL_GQ, COL_GK, COL_MKVA, COL_GV, COL_GO, COL_MQA, COL_MKR, COL_GR = (
    2304, 2560, 2816, 3072, 3584, 4096, 4608, 4672)
MLA_QK = 256
STRIP = 256
ONES_ROWS = 16
FFN_SLAB = 256
GAP_LIMIT = 64.0

TILES = dict(
    rope_rows=1024,
    in_proj_rows=512,
    mla_prep_rows=1024,
    diff_q=1024, diff_k=512,
    mla_q=2048, mla_k=512,
    gla_block=256, gla_chunk=128,
    out_proj_rows=1024,
    ffn_rows=1024, ffn_cols=512,
)


def _cparams(sem):
    return pltpu.CompilerParams(dimension_semantics=sem, vmem_limit_bytes=VMEM_LIMIT)


def _rms(x, g):
    return x * lax.rsqrt(jnp.mean(x * x, axis=-1, keepdims=True) + EPS) * g


def _norm_matmul_kernel(x_ref, g_ref, w_ref, o_ref):
    h = _rms(x_ref[...].astype(F32), g_ref[...]).astype(BF16)
    o_ref[...] = jnp.dot(h, w_ref[...], preferred_element_type=F32).astype(o_ref.dtype)


def _norm_matmul(x, g, w, layer, *, tm):
    t, d = x.shape
    n = w.shape[-1]
    return pl.pallas_call(
        _norm_matmul_kernel,
        grid=(t // tm,),
        in_specs=[
            pl.BlockSpec((tm, d), lambda i: (i, 0)),
            pl.BlockSpec((None, 1, d), lambda i: (layer, 0, 0)),
            pl.BlockSpec((None, d, n), lambda i: (layer, 0, 0), pipeline_mode=pl.Buffered(1)),
        ],
        out_specs=pl.BlockSpec((tm, n), lambda i: (i, 0)),
        out_shape=jax.ShapeDtypeStruct((t, n), BF16),
        compiler_params=_cparams(("parallel",)),
        name="in_proj",
    )(x, g, w)


def _rope_table_kernel(pos_ref, inv_ref, cos_ref, sa_ref, sb_ref):
    ang = pos_ref[...] * inv_ref[...]
    c = jnp.cos(ang)
    s = jnp.sin(ang)
    lane = lax.broadcasted_iota(jnp.int32, ang.shape, 1)
    cos_ref[...] = jnp.where(lane < MLA_ROPE, c, 0.0)
    sa_ref[...] = jnp.where(lane < MLA_ROPE // 2, -s, 0.0)
    sb_ref[...] = jnp.where(lane < MLA_ROPE // 2, 0.0, jnp.where(lane < MLA_ROPE, s, 0.0))


def _rope_tables(posb, inv, *, tm):
    t = posb.shape[0]
    spec = pl.BlockSpec((tm, LANES), lambda i: (i, 0))
    return pl.pallas_call(
        _rope_table_kernel,
        grid=(t // tm,),
        in_specs=[spec, pl.BlockSpec((1, LANES), lambda i: (0, 0))],
        out_specs=[spec, spec, spec],
        out_shape=[jax.ShapeDtypeStruct((t, LANES), F32)] * 3,
        compiler_params=_cparams(("parallel",)),
        name="rope_tables",
    )(posb, inv)


def _rope128(x, cos, sa, sb):
    return x * cos + pltpu.roll(x, 96, 1) * sa + pltpu.roll(x, 32, 1) * sb


def _mla_prep_kernel(qa_ref, kva_ref, kr_ref, gq_ref, gkv_ref, wq_ref, wkv_ref,
                     cos_ref, sa_ref, sb_ref, q_ref, k_ref, v_ref):
    cos, sa, sb = cos_ref[...], sa_ref[...], sb_ref[...]
    scale = (MLA_NOPE + MLA_ROPE) ** -0.5 * LOG2E

    hq = _rms(qa_ref[...].astype(F32), gq_ref[...]).astype(BF16)
    q = jnp.dot(hq, wq_ref[...], preferred_element_type=F32)
    for h in range(MLA_HEADS):
        lo = h * MLA_QK
        q_ref[:, lo:lo + LANES] = (q[:, lo:lo + LANES] * scale).astype(BF16)
        rp = _rope128(q[:, lo + LANES:lo + 2 * LANES], cos, sa, sb)
        q_ref[:, lo + LANES:lo + 2 * LANES] = (rp * scale).astype(BF16)

    hk = _rms(kva_ref[...].astype(F32), gkv_ref[...]).astype(BF16)
    kv = jnp.dot(hk, wkv_ref[...], preferred_element_type=F32)
    lane = lax.broadcasted_iota(jnp.int32, cos.shape, 1)
    kr = jnp.where(lane < MLA_ROPE, kr_ref[...].astype(F32), 0.0)
    krr = _rope128(kr, cos, sa, sb).astype(BF16)
    for h in range(MLA_HEADS):
        lo = h * MLA_QK
        k_ref[:, lo:lo + LANES] = kv[:, h * LANES:(h + 1) * LANES].astype(BF16)
        k_ref[:, lo + LANES:lo + 2 * LANES] = krr
    v_ref[...] = kv[:, MLA_HEADS * LANES:].astype(BF16)


def _mla_prep(proj, gq, gkv, wq, wkv, cos, sa, sb, layer, *, tm):
    t = proj.shape[0]
    nq = MLA_HEADS * MLA_QK
    nv = MLA_HEADS * MLA_VD
    tab = pl.BlockSpec((tm, LANES), lambda i: (i, 0))
    return pl.pallas_call(
        _mla_prep_kernel,
        grid=(t // tm,),
        in_specs=[
            pl.BlockSpec((tm, MLA_Q_RANK), lambda i: (i, COL_MQA // MLA_Q_RANK)),
            pl.BlockSpec((tm, MLA_KV_RANK), lambda i: (i, COL_MKVA // MLA_KV_RANK)),
            pl.BlockSpec((tm, LANES), lambda i: (i, COL_MKR // LANES)),
            pl.BlockSpec((None, 1, MLA_Q_RANK), lambda i: (layer, 0, 0)),
            pl.BlockSpec((None, 1, MLA_KV_RANK), lambda i: (layer, 0, 0)),
            pl.BlockSpec((None, MLA_Q_RANK, nq), lambda i: (layer, 0, 0)),
            pl.BlockSpec((None, MLA_KV_RANK, nq), lambda i: (layer, 0, 0)),
            tab, tab, tab,
        ],
        out_specs=[
            pl.BlockSpec((tm, nq), lambda i: (i, 0)),
            pl.BlockSpec((tm, nq), lambda i: (i, 0)),
            pl.BlockSpec((tm, nv), lambda i: (i, 0)),
        ],
        out_shape=[
            jax.ShapeDtypeStruct((t, nq), BF16),
            jax.ShapeDtypeStruct((t, nq), BF16),
            jax.ShapeDtypeStruct((t, nv), BF16),
        ],
        compiler_params=_cparams(("parallel",)),
        name="mla_prep",
    )(proj, proj, proj, gq, gkv, wq, wkv, cos, sa, sb)


def _attn_kernel(*refs, n_maps, bq, bk, diff, gap_limit):
    if diff:
        q_ref, k_ref, v_ref, lam_ref, cst_ref, subln_ref, o_ref = refs[:7]
    else:
        q_ref, k_ref, v_ref, o_ref = refs[:4]
    qt_s, vt_s, bias_s, p0, p1, al0, al1, r_s, over_s, acc_s = refs[-10:]
    pbuf, abuf = (p0, p1), (al0, al1)
    qi = pl.program_id(2)
    vd = LANES

    @pl.when(qi == 0)
    def _():
        def tr(c, carry):
            off = pl.multiple_of(c * bk, bk)
            vt_s[0:vd, pl.ds(off, bk)] = v_ref[pl.ds(off, bk), :].astype(F32).T.astype(BF16)
            return carry
        lax.fori_loop(0, vt_s.shape[1] // bk, tr, 0)
        vt_s[vd:, :] = jnp.ones((vt_s.shape[0] - vd, vt_s.shape[1]), BF16)
        key = lax.broadcasted_iota(jnp.int32, (bk, STRIP), 0)
        qry = lax.broadcasted_iota(jnp.int32, (bk, STRIP), 1)
        for t in range(bk // STRIP):
            bias_s[t] = jnp.where(key <= qry + t * STRIP, 0.0, -jnp.inf)

    nq = n_maps * bq
    strips = [slice(c * STRIP, (c + 1) * STRIP) for c in range(nq // STRIP)]
    groups = [[strips[g + m * (bq // STRIP)] for m in range(n_maps)] for g in range(bq // STRIP)]
    ratio = bq // bk
    first_diag = ratio * qi
    n_blocks = first_diag + ratio

    def load_queries(group):
        rows = slice(group[0].start, group[0].stop)
        if diff:
            qt = (q_ref[rows, :].astype(F32) * (DIFF_HD ** -0.5 * LOG2E)).T
            row = lax.broadcasted_iota(jnp.int32, qt.shape, 0)
            qt_s[:, group[0]] = jnp.where(row < DIFF_HD, qt, 0.0).astype(BF16)
            qt_s[:, group[1]] = jnp.where(row < DIFF_HD, 0.0, qt).astype(BF16)
        else:
            qt_s[:, group[0]] = q_ref[rows, :].astype(F32).T.astype(BF16)
        for cs in group:
            acc_s[:, cs] = jnp.zeros((acc_s.shape[0], STRIP), F32)

    def store_output(group):
        rows = slice(group[0].start, group[0].stop)
        ot = [acc_s[0:vd, cs] * (1.0 / acc_s[vd:vd + 1, cs]) for cs in group]
        if diff:
            lp = lam_ref[...]
            lam_init = cst_ref[:, 0:1]
            lam = (jnp.exp(jnp.sum(lp[0:1] * lp[1:2], axis=-1, keepdims=True))
                   - jnp.exp(jnp.sum(lp[2:3] * lp[3:4], axis=-1, keepdims=True)) + lam_init)
            d = (ot[0] - lam * ot[1]).T
            o_ref[rows, :] = (_rms(d, subln_ref[...]) * (1.0 - lam_init)).astype(o_ref.dtype)
        else:
            o_ref[rows, :] = ot[0].T.astype(o_ref.dtype)

    def scores(j, masked, cs, rows=bk):
        off = pl.multiple_of(j * bk, bk)
        s = jnp.dot(k_ref[pl.ds(off, rows), :], qt_s[:, cs], preferred_element_type=F32)
        if masked:
            key = lax.broadcasted_iota(jnp.int32, s.shape, 0) + (j * bk - qi * bq)
            qry = jnp.bitwise_and(lax.broadcasted_iota(jnp.int32, s.shape, 1) + cs.start, bq - 1)
            s = jnp.where(key <= qry, s, -jnp.inf)
        return s

    def first_reference(cs):
        mb = jnp.max(scores(0, True, cs, rows=ONES_ROWS), axis=0, keepdims=True)
        r_s[:, cs] = mb
        abuf[0][:, cs] = jnp.ones_like(mb)
        abuf[1][:, cs] = jnp.ones_like(mb)
        over_s[:, cs] = jnp.zeros_like(mb)

    def probs(j, slot, bias, cs):
        rows = bk if bias is None else (bias + 1) * STRIP
        s = scores(j, False, cs, rows=rows)
        if bias is not None:
            s = s + bias_s[bias, 0:rows, :]
        r_old = r_s[:, cs]
        pbuf[slot][0:rows, cs] = jnp.exp2(s - r_old).astype(BF16)
        mb = jnp.max(s, axis=0, keepdims=True)
        r_new = jnp.maximum(r_old, mb)
        abuf[1 - slot][:, cs] = jnp.exp2(r_old - r_new)
        over_s[:, cs] = jnp.maximum(over_s[:, cs], mb - r_old)
        r_s[:, cs] = r_new

    def values(j, slot, cs, rows=bk):
        off = pl.multiple_of(j * bk, bk)
        upd = jnp.dot(vt_s[:, pl.ds(off, rows)], pbuf[slot][0:rows, cs],
                      preferred_element_type=F32)
        acc_s[:, cs] = acc_s[:, cs] * abuf[slot][:, cs] + upd

    def pair(j):
        for cs in strips:
            values(j, 0, cs)
            probs(j + 1, 1, None, cs)
        for cs in strips:
            values(j + 1, 1, cs)
            probs(j + 2, 0, None, cs)

    def two_pairs(u, carry):
        pair(4 * u)
        pair(4 * u + 2)
        return carry

    def visibility(d, cs):
        lo = cs.start % bq
        if lo + STRIP <= d * bk:
            return "none"
        return "all" if lo >= (d + 1) * bk else "part"

    def diag_bias(d, cs):
        return (cs.start % bq - d * bk) // STRIP if visibility(d, cs) == "part" else None

    def diag_probs(d, cs):
        if visibility(d, cs) != "none":
            probs(first_diag + d, d % 2, diag_bias(d, cs), cs)

    def diag_values(d, cs):
        if visibility(d, cs) != "none":
            t = diag_bias(d, cs)
            values(first_diag + d, d % 2, cs, rows=bk if t is None else (t + 1) * STRIP)

    def first_block(diagonal):
        for group in groups:
            load_queries(group)
            for cs in group:
                first_reference(cs)
                probs(0, 0, diag_bias(0, cs) if diagonal else None, cs)

    @pl.when(qi == 0)
    def _():
        first_block(True)

    @pl.when(qi >= 1)
    def _():
        first_block(False)

    n_pairs = jnp.maximum(first_diag // 2 - 1, 0)
    lax.fori_loop(0, n_pairs // 2, two_pairs, 0)

    @pl.when(jnp.bitwise_and(n_pairs, 1) == 1)
    def _():
        pair(2 * (n_pairs - 1))

    def diagonal_blocks():
        for d in range(1, ratio):
            for cs in strips:
                diag_values(d - 1, cs)
                diag_probs(d, cs)
        for cs in strips:
            diag_values(ratio - 1, cs)
        for group in groups:
            store_output(group)

    @pl.when(qi >= 1)
    def _():
        for cs in strips:
            values(first_diag - 2, 0, cs)
            probs(first_diag - 1, 1, None, cs)
        for cs in strips:
            values(first_diag - 1, 1, cs)
            diag_probs(0, cs)
        diagonal_blocks()

    @pl.when(qi == 0)
    def _():
        diagonal_blocks()

    @pl.when(jnp.max(over_s[...]) > gap_limit)
    def _():
        r_s[...] = jnp.full(r_s.shape, -jnp.inf, F32)
        acc_s[...] = jnp.zeros(acc_s.shape, F32)

        def exact(j, carry):
            off = pl.multiple_of(j * bk, bk)
            for cs in strips:
                s = scores(j, True, cs)
                m_old = r_s[:, cs]
                m_new = jnp.maximum(m_old, jnp.max(s, axis=0, keepdims=True))
                p = jnp.exp2(s - m_new).astype(BF16)
                upd = jnp.dot(vt_s[:, pl.ds(off, bk)], p, preferred_element_type=F32)
                acc_s[:, cs] = acc_s[:, cs] * jnp.exp2(m_old - m_new) + upd
                r_s[:, cs] = m_new
            return carry

        lax.fori_loop(0, n_blocks, exact, 0)
        for group in groups:
            store_output(group)


def _attention(q, k, v, extras, *, batch, seq, heads, qk_dim, q_col, k_col, v_col, bq, bk, diff, name,
               gap_limit=GAP_LIMIT):
    n_maps = 2 if diff else 1
    nq = n_maps * bq
    nqb = seq // bq
    assert bq % (2 * bk) == 0 and bk % STRIP == 0 and seq % bq == 0
    kernel = functools.partial(_attn_kernel, n_maps=n_maps, bq=bq, bk=bk, diff=diff, gap_limit=gap_limit)
    in_specs = [
        pl.BlockSpec((bq, qk_dim), lambda b, h, i: (b * nqb + i, q_col + h)),
        pl.BlockSpec((seq, qk_dim), lambda b, h, i: (b, k_col + h)),
        pl.BlockSpec((seq, LANES), lambda b, h, i: (b, v_col + h)),
    ]
    in_specs += [spec for _, spec in extras]
    row = pltpu.VMEM((1, nq), F32)
    return pl.pallas_call(
        kernel,
        grid=(batch, heads, nqb),
        in_specs=in_specs,
        out_specs=pl.BlockSpec((bq, LANES), lambda b, h, i: (b * nqb + i, h)),
        out_shape=jax.ShapeDtypeStruct((batch * seq, heads * LANES), BF16),
        scratch_shapes=[
            pltpu.VMEM((qk_dim, nq), BF16),
            pltpu.VMEM((LANES + ONES_ROWS, seq), BF16),
            pltpu.VMEM((bk // STRIP, bk, STRIP), F32),
            pltpu.VMEM((bk, nq), BF16), pltpu.VMEM((bk, nq), BF16),
            row, row, row, row,
            pltpu.VMEM((LANES + ONES_ROWS, nq), F32),
        ],
        compiler_params=_cparams(("parallel", "parallel", "arbitrary")),
        name=name,
    )(q, k, v, *[a for a, _ in extras])


def _gla_kernel(gq_ref, gk_ref, gv_ref, go_ref, gr_ref, gw_ref, gb_ref, gn_ref, o_ref, st_ref, *,
                block, chunk):
    c = chunk

    @pl.when(pl.program_id(1) == 0)
    def _():
        st_ref[...] = jnp.zeros(st_ref.shape, F32)

    logits = jnp.dot(gr_ref[...], gw_ref[...], preferred_element_type=F32) + gb_ref[...]
    soft = jnp.log2(1.0 + jnp.exp2(jnp.abs(logits) * (-LOG2E)))
    la = (jnp.minimum(logits, 0.0) * LOG2E - soft) * (1.0 / GLA_TAU)

    row = lax.broadcasted_iota(jnp.int32, la.shape, 0)
    in_chunk = jnp.bitwise_and(row, c - 1)
    b_all = la
    s = 1
    while s < c:
        b_all = b_all + jnp.where(in_chunk >= s, pltpu.roll(b_all, s, 0), 0.0)
        s *= 2

    lane = lax.broadcasted_iota(jnp.int32, (c, LANES), 1)
    first_head = lane < GLA_DK
    crow = lax.broadcasted_iota(jnp.int32, (c, GLA_HEADS * GLA_DK), 0)
    ri = lax.broadcasted_iota(jnp.int32, (c, c), 0)
    ci = lax.broadcasted_iota(jnp.int32, (c, c), 1)
    lvl = jnp.where(ri > ci, jnp.bitwise_xor(ri, ci), 0)
    diag = ri == ci
    n_levels = c.bit_length() - 1
    level_masks = [lax.shift_right_logical(lvl, bit) == 1 for bit in range(n_levels)]
    er = lax.broadcasted_iota(jnp.int32, (2 * GLA_DV, LANES), 0)
    ec = lax.broadcasted_iota(jnp.int32, (2 * GLA_DV, LANES), 1)
    bd_mask = (er < GLA_DV) == (ec < GLA_DK)
    gn = gn_ref[...]

    def head_scores(qt, kt):
        out = []
        for p in range(GLA_HEADS // 2):
            qp = qt[:, p * LANES:(p + 1) * LANES]
            kp = kt[:, p * LANES:(p + 1) * LANES].astype(BF16)
            for hh in range(2):
                qh = jnp.where(first_head if hh == 0 else jnp.logical_not(first_head), qp, 0.0)
                out.append(lax.dot_general(qh.astype(BF16), kp, (((1,), (1,)), ((), ())),
                                           preferred_element_type=F32))
        return out

    for sc in range(block // c):
        rows = slice(sc * c, (sc + 1) * c)
        b = b_all[rows]
        q = gq_ref[rows, :].astype(F32) * (GLA_DK ** -0.5)
        k = gk_ref[rows, :].astype(F32)
        v = gv_ref[rows, :]

        attn = [jnp.where(diag, m, 0.0) for m in head_scores(q, k)]
        ref_b = b
        for bit in range(n_levels):
            hs = 1 << bit
            if hs > 1:
                ref_b = jnp.where(jnp.bitwise_and(crow, hs - 1) < hs // 2, ref_b,
                                  pltpu.roll(ref_b, hs // 2, 0))
            qt = q * jnp.exp2(jnp.minimum(b - ref_b, 0.0))
            nxt = pltpu.roll(ref_b, c - hs, 0)
            kt = k * jnp.exp2(jnp.minimum(nxt - b, 0.0))
            attn = [jnp.where(level_masks[bit], m, a) for m, a in zip(head_scores(qt, kt), attn)]

        b_end = b[c - 1:c, :]
        qb = (q * jnp.exp2(b)).astype(BF16)
        kd = (k * jnp.exp2(b_end - b)).astype(BF16)

        for p in range(GLA_HEADS // 2):
            st = st_ref[p]
            inter = lax.dot_general(qb[:, p * LANES:(p + 1) * LANES], st.astype(BF16),
                                    (((1,), (1,)), ((), ())), preferred_element_type=F32)
            vp = v[:, p * 2 * GLA_DV:(p + 1) * 2 * GLA_DV]
            for hh in range(2):
                h = 2 * p + hh
                cols = slice(h * GLA_DV, (h + 1) * GLA_DV)
                intra = jnp.dot(attn[h].astype(BF16), vp[:, hh * GLA_DV:(hh + 1) * GLA_DV],
                                preferred_element_type=F32)
                out = inter[:, hh * GLA_DV:(hh + 1) * GLA_DV] + intra
                g = go_ref[rows, cols].astype(F32)
                o_ref[rows, cols] = (_rms(out, gn) * (g / (1.0 + jnp.exp(-g)))).astype(o_ref.dtype)
            upd = lax.dot_general(vp, kd[:, p * LANES:(p + 1) * LANES], (((0,), (0,)), ((), ())),
                                  preferred_element_type=F32)
            st_ref[p] = (st * jnp.exp2(b_end[:, p * LANES:(p + 1) * LANES])
                         + jnp.where(bd_mask, upd, 0.0))


def _gla(proj, gw, gb, gn, layer, *, batch, seq, block, chunk):
    nc = seq // block
    qk = GLA_HEADS * GLA_DK
    vd = GLA_HEADS * GLA_DV
    assert block % chunk == 0 and chunk & (chunk - 1) == 0
    return pl.pallas_call(
        functools.partial(_gla_kernel, block=block, chunk=chunk),
        grid=(batch, nc),
        in_specs=[
            pl.BlockSpec((block, qk), lambda b, c: (b * nc + c, COL_GQ // qk)),
            pl.BlockSpec((block, qk), lambda b, c: (b * nc + c, COL_GK // qk)),
            pl.BlockSpec((block, vd), lambda b, c: (b * nc + c, COL_GV // vd)),
            pl.BlockSpec((block, vd), lambda b, c: (b * nc + c, COL_GO // vd)),
            pl.BlockSpec((block, LANES), lambda b, c: (b * nc + c, COL_MKR // LANES)),
            pl.BlockSpec((None, LANES, qk), lambda b, c: (layer, 0, 0)),
            pl.BlockSpec((None, 1, qk), lambda b, c: (layer, 0, 0)),
            pl.BlockSpec((None, 1, GLA_DV), lambda b, c: (layer, 0, 0)),
        ],
        out_specs=pl.BlockSpec((block, vd), lambda b, c: (b * nc + c, 0)),
        out_shape=jax.ShapeDtypeStruct((batch * seq, vd), BF16),
        scratch_shapes=[pltpu.VMEM((GLA_HEADS // 2, 2 * GLA_DV, LANES), F32)],
        compiler_params=_cparams(("parallel", "arbitrary")),
        name="gla",
    )(proj, proj, proj, proj, proj, gw, gb, gn)


def _out_proj_kernel(x_ref, a_ref, b_ref, c_ref, w_ref, o_ref):
    na, nb = a_ref.shape[1], b_ref.shape[1]
    acc = jnp.dot(a_ref[...], w_ref[0:na, :], preferred_element_type=F32)
    acc += jnp.dot(b_ref[...], w_ref[na:na + nb, :], preferred_element_type=F32)
    acc += jnp.dot(c_ref[...], w_ref[na + nb:, :], preferred_element_type=F32)
    o_ref[...] = x_ref[...] + acc


def _out_proj(x, oa, ob, oc, w, layer, *, tm):
    t, d = x.shape

    def act(a):
        return pl.BlockSpec((tm, a.shape[1]), lambda i: (i, 0))

    return pl.pallas_call(
        _out_proj_kernel,
        grid=(t // tm,),
        in_specs=[act(x), act(oa), act(ob), act(oc),
                  pl.BlockSpec((None, w.shape[1], d), lambda i: (layer, 0, 0),
                               pipeline_mode=pl.Buffered(1))],
        out_specs=pl.BlockSpec((tm, d), lambda i: (i, 0)),
        out_shape=jax.ShapeDtypeStruct((t, d), F32),
        compiler_params=_cparams(("parallel",)),
        name="out_proj",
    )(x, oa, ob, oc, w)


def _ffn_kernel(x_ref, g_ref, wg_ref, wu_ref, wd_ref, fg_ref, o_ref, h_ref, *, final_norm):
    f = pl.program_id(1)

    def swiglu(h):
        gate = jnp.dot(h, wg_ref[...], preferred_element_type=F32)
        up = jnp.dot(h, wu_ref[...], preferred_element_type=F32)
        act = (gate / (1.0 + jnp.exp(-gate)) * up).astype(BF16)
        return jnp.dot(act, wd_ref[...], preferred_element_type=F32)

    @pl.when(f == 0)
    def _():
        for s in range(x_ref.shape[0] // FFN_SLAB):
            rows = slice(s * FFN_SLAB, (s + 1) * FFN_SLAB)
            x = x_ref[rows, :]
            h = _rms(x, g_ref[...]).astype(BF16)
            h_ref[rows, :] = h
            o_ref[rows, :] = x + swiglu(h)

    @pl.when(f > 0)
    def _():
        o_ref[...] += swiglu(h_ref[...])

    if final_norm:
        @pl.when(f == pl.num_programs(1) - 1)
        def _():
            o_ref[...] = _rms(o_ref[...], fg_ref[...])


def _ffn(x, g, wg, wu, wd, fg, layer, *, tm, tf, final_norm):
    t, d = x.shape
    ff = wg.shape[-1]
    return pl.pallas_call(
        functools.partial(_ffn_kernel, final_norm=final_norm),
        grid=(t // tm, ff // tf),
        in_specs=[
            pl.BlockSpec((tm, d), lambda i, f: (i, 0)),
            pl.BlockSpec((None, 1, d), lambda i, f: (layer, 0, 0)),
            pl.BlockSpec((None, d, tf), lambda i, f: (layer, 0, f)),
            pl.BlockSpec((None, d, tf), lambda i, f: (layer, 0, f)),
            pl.BlockSpec((None, tf, d), lambda i, f: (layer, f, 0)),
            pl.BlockSpec((1, d), lambda i, f: (0, 0)),
        ],
        out_specs=pl.BlockSpec((tm, d), lambda i, f: (i, 0)),
        out_shape=jax.ShapeDtypeStruct((t, d), F32),
        scratch_shapes=[pltpu.VMEM((tm, d), BF16)],
        compiler_params=_cparams(("parallel", "arbitrary")),
        name="ffn",
    )(x, g, wg, wu, wd, fg)


def _prep_w_in(w_in):
    sizes = (768, 768, 768, 256, 256, 512, 16, 512, 512, 256, 64)
    offs = [0]
    for s in sizes:
        offs.append(offs[-1] + s)
    d_q, d_k, d_v, g_q, g_k, g_v, g_r, g_o, m_qa, m_kva, m_kr = (
        w_in[..., offs[i]:offs[i + 1]] for i in range(len(sizes)))
    used = COL_GR + GLA_GATE_RANK
    pad = jnp.zeros(w_in.shape[:-1] + (PROJ_COLS - used,), w_in.dtype)
    parts = [d_q, d_k, d_v, g_q, g_k, m_kva, g_v, g_o, m_qa, m_kr, g_r, pad]
    return jnp.concatenate([p.astype(BF16) for p in parts], axis=-1)


def _prep_w_q_b(w):
    depth, r, _ = w.shape
    w = w.reshape(depth, r, MLA_HEADS, MLA_NOPE + MLA_ROPE)
    w = jnp.pad(w, ((0, 0), (0, 0), (0, 0), (0, MLA_QK - MLA_NOPE - MLA_ROPE)))
    return w.reshape(depth, r, MLA_HEADS * MLA_QK).astype(BF16)


def _prep_w_kv_b(w):
    depth, r, _ = w.shape
    w = w.reshape(depth, r, MLA_HEADS, MLA_NOPE + MLA_VD)
    k = w[..., :MLA_NOPE].reshape(depth, r, MLA_HEADS * MLA_NOPE)
    v = w[..., MLA_NOPE:].reshape(depth, r, MLA_HEADS * MLA_VD)
    return jnp.concatenate([k, v], axis=-1).astype(BF16)


def _prep_gate_w(w):
    depth = w.shape[0]
    lo = COL_GR - COL_MKR
    out = jnp.zeros((depth, LANES, w.shape[-1]), w.dtype)
    return out.at[:, lo:lo + GLA_GATE_RANK, :].set(w).astype(BF16)


def kernel(x, positions, attn_norm, w_in, diff_lambda_q1, diff_lambda_k1, diff_lambda_q2, diff_lambda_k2, diff_subln, gla_gate_w, gla_gate_b, gla_norm, mla_q_norm, mla_w_q_b, mla_kv_norm, mla_w_kv_b, w_out, ffn_norm, w_gate, w_up, w_down, final_norm):
    batch, seq, d = x.shape
    depth = w_in.shape[0]
    t = batch * seq
    assert d == D_MODEL and seq % max(TILES["diff_q"], TILES["mla_q"]) == 0

    w_in_p = _prep_w_in(w_in)
    wq_p = _prep_w_q_b(mla_w_q_b)
    wkv_p = _prep_w_kv_b(mla_w_kv_b)
    gw_p = _prep_gate_w(gla_gate_w)
    wo_p = w_out.astype(BF16)
    wg_p, wu_p, wd_p = w_gate.astype(BF16), w_up.astype(BF16), w_down.astype(BF16)
    lam_p = jnp.stack([diff_lambda_q1, diff_lambda_k1, diff_lambda_q2, diff_lambda_k2], axis=1).astype(F32)
    lam_init = jnp.asarray([0.8 - 0.6 * math.exp(-0.3 * l) for l in range(depth)], F32)
    cst = jnp.zeros((depth, 1, LANES), F32).at[:, 0, 0].set(lam_init)

    def row3(a):
        return a.astype(F32)[:, None, :]

    posb = jnp.broadcast_to(positions.astype(F32).reshape(t, 1), (t, LANES))
    half = MLA_ROPE // 2
    inv = 1.0 / (ROPE_THETA ** (jnp.arange(0, MLA_ROPE, 2, dtype=F32) / MLA_ROPE))
    inv = jnp.concatenate([inv, inv, jnp.zeros((LANES - 2 * half,), F32)])[None, :]
    cos, sa, sb = _rope_tables(posb, inv, tm=TILES["rope_rows"])

    xs = x.reshape(t, d)
    for l in range(depth):
        proj = _norm_matmul(xs, row3(attn_norm), w_in_p, l, tm=TILES["in_proj_rows"])
        mq, mk, mv = _mla_prep(proj, row3(mla_q_norm), row3(mla_kv_norm), wq_p, wkv_p, cos, sa, sb, l,
                               tm=TILES["mla_prep_rows"])
        diff_extras = [
            (lam_p, pl.BlockSpec((None, 4, DIFF_HD), lambda b, h, i, l=l: (l, 0, 0))),
            (cst, pl.BlockSpec((None, 1, LANES), lambda b, h, i, l=l: (l, 0, 0))),
            (row3(diff_subln), pl.BlockSpec((None, 1, DIFF_VD), lambda b, h, i, l=l: (l, 0, 0))),
        ]
        o_diff = _attention(proj, proj, proj, diff_extras, batch=batch, seq=seq, heads=DIFF_HEADS,
                            qk_dim=LANES, q_col=COL_DQ // LANES, k_col=COL_DK // LANES,
                            v_col=COL_DV // LANES, bq=TILES["diff_q"], bk=TILES["diff_k"], diff=True,
                            name="diff_attn")
        o_gla = _gla(proj, gw_p, row3(gla_gate_b), row3(gla_norm), l, batch=batch, seq=seq,
                     block=TILES["gla_block"], chunk=TILES["gla_chunk"])
        o_mla = _attention(mq, mk, mv, [], batch=batch, seq=seq, heads=MLA_HEADS,
                           qk_dim=MLA_QK, q_col=0, k_col=0, v_col=0, bq=TILES["mla_q"],
                           bk=TILES["mla_k"], diff=False,
                           name="mla_attn")
        x1 = _out_proj(xs, o_diff, o_gla, o_mla, wo_p, l, tm=TILES["out_proj_rows"])
        xs = _ffn(x1, row3(ffn_norm), wg_p, wu_p, wd_p, final_norm.astype(F32)[None, :], l,
                  tm=TILES["ffn_rows"], tf=TILES["ffn_cols"], final_norm=(l == depth - 1))
    return xs.reshape(batch, seq, d)
```

```python
import functools
import math

import jax
import jax.numpy as jnp
from jax import lax
from jax.experimental import pallas as pl
from jax.experimental.pallas import tpu as pltpu

F32 = jnp.float32
BF16 = jnp.bfloat16

D_MODEL = 2048
DIFF_HEADS = 6
DIFF_HD = 64
DIFF_VD = 128
GLA_HEADS = 4
GLA_DK = 64
GLA_DV = 128
GLA_GATE_RANK = 16
GLA_TAU = 16.0
MLA_HEADS = 6
MLA_Q_RANK = 512
MLA_KV_RANK = 256
MLA_NOPE = 128
MLA_ROPE = 64
MLA_VD = 128
ROPE_THETA = 10000.0
D_FF = 5632
EPS = 1e-6
LOG2E = 1.4426950408889634

LANES = 128
VMEM_LIMIT = 56 * 1024 * 1024

PROJ_COLS = 4736
COL_DQ, COL_DK, COL_DV = 0, 768, 1536
COL_GQ, COL_GK, COL_MKVA, COL_GV, COL_GO, COL_MQA, COL_MKR, COL_GR = (
    2304, 2560, 2816, 3072, 3584, 4096, 4608, 4672)
MLA_QK = 256
STRIP = 256
ONES_ROWS = 16
FFN_SLAB = 256
GAP_LIMIT = 64.0

TILES = dict(
    rope_rows=1024,
    in_proj_rows=512,
    mla_prep_rows=1024,
    diff_q=1024, diff_k=512,
    mla_q=2048, mla_k=512,
    gla_block=256, gla_chunk=128,
    out_proj_rows=1024,
    ffn_rows=1024, ffn_cols=512,
)


def _cparams(sem):
    return pltpu.CompilerParams(dimension_semantics=sem, vmem_limit_bytes=VMEM_LIMIT)


def _rms(x, g):
    return x * lax.rsqrt(jnp.mean(x * x, axis=-1, keepdims=True) + EPS) * g


def _norm_matmul_kernel(x_ref, g_ref, w_ref, o_ref):
    h = _rms(x_ref[...].astype(F32), g_ref[...]).astype(BF16)
    o_ref[...] = jnp.dot(h, w_ref[...], preferred_element_type=F32).astype(o_ref.dtype)


def _norm_matmul(x, g, w, layer, *, tm):
    t, d = x.shape
    n = w.shape[-1]
    return pl.pallas_call(
        _norm_matmul_kernel,
        grid=(t // tm,),
        in_specs=[
            pl.BlockSpec((tm, d), lambda i: (i, 0)),
            pl.BlockSpec((None, 1, d), lambda i: (layer, 0, 0)),
            pl.BlockSpec((None, d, n), lambda i: (layer, 0, 0), pipeline_mode=pl.Buffered(1)),
        ],
        out_specs=pl.BlockSpec((tm, n), lambda i: (i, 0)),
        out_shape=jax.ShapeDtypeStruct((t, n), BF16),
        compiler_params=_cparams(("parallel",)),
        name="in_proj",
    )(x, g, w)


def _rope_table_kernel(pos_ref, inv_ref, cos_ref, sa_ref, sb_ref):
    ang = pos_ref[...] * inv_ref[...]
    c = jnp.cos(ang)
    s = jnp.sin(ang)
    lane = lax.broadcasted_iota(jnp.int32, ang.shape, 1)
    cos_ref[...] = jnp.where(lane < MLA_ROPE, c, 0.0)
    sa_ref[...] = jnp.where(lane < MLA_ROPE // 2, -s, 0.0)
    sb_ref[...] = jnp.where(lane < MLA_ROPE // 2, 0.0, jnp.where(lane < MLA_ROPE, s, 0.0))


def _rope_tables(posb, inv, *, tm):
    t = posb.shape[0]
    spec = pl.BlockSpec((tm, LANES), lambda i: (i, 0))
    return pl.pallas_call(
        _rope_table_kernel,
        grid=(t // tm,),
        in_specs=[spec, pl.BlockSpec((1, LANES), lambda i: (0, 0))],
        out_specs=[spec, spec, spec],
        out_shape=[jax.ShapeDtypeStruct((t, LANES), F32)] * 3,
        compiler_params=_cparams(("parallel",)),
        name="rope_tables",
    )(posb, inv)


def _rope128(x, cos, sa, sb):
    return x * cos + pltpu.roll(x, 96, 1) * sa + pltpu.roll(x, 32, 1) * sb


def _mla_prep_kernel(qa_ref, kva_ref, kr_ref, gq_ref, gkv_ref, wq_ref, wkv_ref,
                     cos_ref, sa_ref, sb_ref, q_ref, k_ref, v_ref):
    cos, sa, sb = cos_ref[...], sa_ref[...], sb_ref[...]
    scale = (MLA_NOPE + MLA_ROPE) ** -0.5 * LOG2E

    hq = _rms(qa_ref[...].astype(F32), gq_ref[...]).astype(BF16)
    q = jnp.dot(hq, wq_ref[...], preferred_element_type=F32)
    for h in range(MLA_HEADS):
        lo = h * MLA_QK
        q_ref[:, lo:lo + LANES] = (q[:, lo:lo + LANES] * scale).astype(BF16)
        rp = _rope128(q[:, lo + LANES:lo + 2 * LANES], cos, sa, sb)
        q_ref[:, lo + LANES:lo + 2 * LANES] = (rp * scale).astype(BF16)

    hk = _rms(kva_ref[...].astype(F32), gkv_ref[...]).astype(BF16)
    kv = jnp.dot(hk, wkv_ref[...], preferred_element_type=F32)
    lane = lax.broadcasted_iota(jnp.int32, cos.shape, 1)
    kr = jnp.where(lane < MLA_ROPE, kr_ref[...].astype(F32), 0.0)
    krr = _rope128(kr, cos, sa, sb).astype(BF16)
    for h in range(MLA_HEADS):
        lo = h * MLA_QK
        k_ref[:, lo:lo + LANES] = kv[:, h * LANES:(h + 1) * LANES].astype(BF16)
        k_ref[:, lo + LANES:lo + 2 * LANES] = krr
    v_ref[...] = kv[:, MLA_HEADS * LANES:].astype(BF16)


def _mla_prep(proj, gq, gkv, wq, wkv, cos, sa, sb, layer, *, tm):
    t = proj.shape[0]
    nq = MLA_HEADS * MLA_QK
    nv = MLA_HEADS * MLA_VD
    tab = pl.BlockSpec((tm, LANES), lambda i: (i, 0))
    return pl.pallas_call(
        _mla_prep_kernel,
        grid=(t // tm,),
        in_specs=[
            pl.BlockSpec((tm, MLA_Q_RANK), lambda i: (i, COL_MQA // MLA_Q_RANK)),
            pl.BlockSpec((tm, MLA_KV_RANK), lambda i: (i, COL_MKVA // MLA_KV_RANK)),
            pl.BlockSpec((tm, LANES), lambda i: (i, COL_MKR // LANES)),
            pl.BlockSpec((None, 1, MLA_Q_RANK), lambda i: (layer, 0, 0)),
            pl.BlockSpec((None, 1, MLA_KV_RANK), lambda i: (layer, 0, 0)),
            pl.BlockSpec((None, MLA_Q_RANK, nq), lambda i: (layer, 0, 0)),
            pl.BlockSpec((None, MLA_KV_RANK, nq), lambda i: (layer, 0, 0)),
            tab, tab, tab,
        ],
        out_specs=[
            pl.BlockSpec((tm, nq), lambda i: (i, 0)),
            pl.BlockSpec((tm, nq), lambda i: (i, 0)),
            pl.BlockSpec((tm, nv), lambda i: (i, 0)),
        ],
        out_shape=[
            jax.ShapeDtypeStruct((t, nq), BF16),
            jax.ShapeDtypeStruct((t, nq), BF16),
            jax.ShapeDtypeStruct((t, nv), BF16),
        ],
        compiler_params=_cparams(("parallel",)),
        name="mla_prep",
    )(proj, proj, proj, gq, gkv, wq, wkv, cos, sa, sb)


def _attn_kernel(*refs, n_maps, bq, bk, diff, gap_limit):
    if diff:
        q_ref, k_ref, v_ref, lam_ref, cst_ref, subln_ref, o_ref = refs[:7]
    else:
        q_ref, k_ref, v_ref, o_ref = refs[:4]
    qt_s, vt_s, bias_s, p0, p1, al0, al1, r_s, over_s, acc_s = refs[-10:]
    pbuf, abuf = (p0, p1), (al0, al1)
    qi = pl.program_id(2)
    vd = LANES

    @pl.when(qi == 0)
    def _():
        def tr(c, carry):
            off = pl.multiple_of(c * bk, bk)
            vt_s[0:vd, pl.ds(off, bk)] = v_ref[pl.ds(off, bk), :].astype(F32).T.astype(BF16)
            return carry
        lax.fori_loop(0, vt_s.shape[1] // bk, tr, 0)
        vt_s[vd:, :] = jnp.ones((vt_s.shape[0] - vd, vt_s.shape[1]), BF16)
        key = lax.broadcasted_iota(jnp.int32, (bk, STRIP), 0)
        qry = lax.broadcasted_iota(jnp.int32, (bk, STRIP), 1)
        for t in range(bk // STRIP):
            bias_s[t] = jnp.where(key <= qry + t * STRIP, 0.0, -jnp.inf)

    nq = n_maps * bq
    strips = [slice(c * STRIP, (c + 1) * STRIP) for c in range(nq // STRIP)]
    groups = [[strips[g + m * (bq // STRIP)] for m in range(n_maps)] for g in range(bq // STRIP)]
    ratio = bq // bk
    first_diag = ratio * qi
    n_blocks = first_diag + ratio

    def load_queries(group):
        rows = slice(group[0].start, group[0].stop)
        if diff:
            qt = (q_ref[rows, :].astype(F32) * (DIFF_HD ** -0.5 * LOG2E)).T
            row = lax.broadcasted_iota(jnp.int32, qt.shape, 0)
            qt_s[:, group[0]] = jnp.where(row < DIFF_HD, qt, 0.0).astype(BF16)
            qt_s[:, group[1]] = jnp.where(row < DIFF_HD, 0.0, qt).astype(BF16)
        else:
            qt_s[:, group[0]] = q_ref[rows, :].astype(F32).T.astype(BF16)
        for cs in group:
            acc_s[:, cs] = jnp.zeros((acc_s.shape[0], STRIP), F32)

    def store_output(group):
        rows = slice(group[0].start, group[0].stop)
        ot = [acc_s[0:vd, cs] * (1.0 / acc_s[vd:vd + 1, cs]) for cs in group]
        if diff:
            lp = lam_ref[...]
            lam_init = cst_ref[:, 0:1]
            lam = (jnp.exp(jnp.sum(lp[0:1] * lp[1:2], axis=-1, keepdims=True))
                   - jnp.exp(jnp.sum(lp[2:3] * lp[3:4], axis=-1, keepdims=True)) + lam_init)
            d = (ot[0] - lam * ot[1]).T
            o_ref[rows, :] = (_rms(d, subln_ref[...]) * (1.0 - lam_init)).astype(o_ref.dtype)
        else:
            o_ref[rows, :] = ot[0].T.astype(o_ref.dtype)

    def scores(j, masked, cs, rows=bk):
        off = pl.multiple_of(j * bk, bk)
        s = jnp.dot(k_ref[pl.ds(off, rows), :], qt_s[:, cs], preferred_element_type=F32)
        if masked:
            key = lax.broadcasted_iota(jnp.int32, s.shape, 0) + (j * bk - qi * bq)
            qry = jnp.bitwise_and(lax.broadcasted_iota(jnp.int32, s.shape, 1) + cs.start, bq - 1)
            s = jnp.where(key <= qry, s, -jnp.inf)
        return s

    def first_reference(cs):
        mb = jnp.max(scores(0, True, cs, rows=ONES_ROWS), axis=0, keepdims=True)
        r_s[:, cs] = mb
        abuf[0][:, cs] = jnp.ones_like(mb)
        abuf[1][:, cs] = jnp.ones_like(mb)
        over_s[:, cs] = jnp.zeros_like(mb)

    def probs(j, slot, bias, cs):
        rows = bk if bias is None else (bias + 1) * STRIP
        s = scores(j, False, cs, rows=rows)
        if bias is not None:
            s = s + bias_s[bias, 0:rows, :]
        r_old = r_s[:, cs]
        pbuf[slot][0:rows, cs] = jnp.exp2(s - r_old).astype(BF16)
        mb = jnp.max(s, axis=0, keepdims=True)
        r_new = jnp.maximum(r_old, mb)
        abuf[1 - slot][:, cs] = jnp.exp2(r_old - r_new)
        over_s[:, cs] = jnp.maximum(over_s[:, cs], mb - r_old)
        r_s[:, cs] = r_new

    def values(j, slot, cs, rows=bk):
        off = pl.multiple_of(j * bk, bk)
        upd = jnp.dot(vt_s[:, pl.ds(off, rows)], pbuf[slot][0:rows, cs],
                      preferred_element_type=F32)
        acc_s[:, cs] = acc_s[:, cs] * abuf[slot][:, cs] + upd

    def pair(j):
        for cs in strips:
            values(j, 0, cs)
            probs(j + 1, 1, None, cs)
        for cs in strips:
            values(j + 1, 1, cs)
            probs(j + 2, 0, None, cs)

    def two_pairs(u, carry):
        pair(4 * u)
        pair(4 * u + 2)
        return carry

    def visibility(d, cs):
        lo = cs.start % bq
        if lo + STRIP <= d * bk:
            return "none"
        return "all" if lo >= (d + 1) * bk else "part"

    def diag_bias(d, cs):
        return (cs.start % bq - d * bk) // STRIP if visibility(d, cs) == "part" else None

    def diag_probs(d, cs):
        if visibility(d, cs) != "none":
            probs(first_diag + d, d % 2, diag_bias(d, cs), cs)

    def diag_values(d, cs):
        if visibility(d, cs) != "none":
            t = diag_bias(d, cs)
            values(first_diag + d, d % 2, cs, rows=bk if t is None else (t + 1) * STRIP)

    def first_block(diagonal):
        for group in groups:
            load_queries(group)
            for cs in group:
                first_reference(cs)
                probs(0, 0, diag_bias(0, cs) if diagonal else None, cs)

    @pl.when(qi == 0)
    def _():
        first_block(True)

    @pl.when(qi >= 1)
    def _():
        first_block(False)

    n_pairs = jnp.maximum(first_diag // 2 - 1, 0)
    lax.fori_loop(0, n_pairs // 2, two_pairs, 0)
    odd_pair = jnp.bitwise_and(n_pairs, 1) == 1

    def diagonal_blocks():
        for d in range(1, ratio):
            for cs in strips:
                diag_values(d - 1, cs)
                diag_probs(d, cs)
        for cs in strips:
            diag_values(ratio - 1, cs)
        for group in groups:
            store_output(group)

    def last_blocks(with_pair):
        if with_pair:
            pair(2 * (n_pairs - 1))
        for cs in strips:
            values(first_diag - 2, 0, cs)
            probs(first_diag - 1, 1, None, cs)
        for cs in strips:
            values(first_diag - 1, 1, cs)
            diag_probs(0, cs)
        diagonal_blocks()

    @pl.when(jnp.logical_and(qi >= 1, odd_pair))
    def _():
        last_blocks(True)

    @pl.when(jnp.logical_and(qi >= 1, jnp.logical_not(odd_pair)))
    def _():
        last_blocks(False)

    @pl.when(qi == 0)
    def _():
        diagonal_blocks()

    @pl.when(jnp.max(over_s[...]) > gap_limit)
    def _():
        r_s[...] = jnp.full(r_s.shape, -jnp.inf, F32)
        acc_s[...] = jnp.zeros(acc_s.shape, F32)

        def exact(j, carry):
            off = pl.multiple_of(j * bk, bk)
            for cs in strips:
                s = scores(j, True, cs)
                m_old = r_s[:, cs]
                m_new = jnp.maximum(m_old, jnp.max(s, axis=0, keepdims=True))
                p = jnp.exp2(s - m_new).astype(BF16)
                upd = jnp.dot(vt_s[:, pl.ds(off, bk)], p, preferred_element_type=F32)
                acc_s[:, cs] = acc_s[:, cs] * jnp.exp2(m_old - m_new) + upd
                r_s[:, cs] = m_new
            return carry

        lax.fori_loop(0, n_blocks, exact, 0)
        for group in groups:
            store_output(group)


def _attention(q, k, v, extras, *, batch, seq, heads, qk_dim, q_col, k_col, v_col, bq, bk, diff, name,
               gap_limit=GAP_LIMIT):
    n_maps = 2 if diff else 1
    nq = n_maps * bq
    nqb = seq // bq
    assert bq % (2 * bk) == 0 and bk % STRIP == 0 and seq % bq == 0
    kernel = functools.partial(_attn_kernel, n_maps=n_maps, bq=bq, bk=bk, diff=diff, gap_limit=gap_limit)
    in_specs = [
        pl.BlockSpec((bq, qk_dim), lambda b, h, i: (b * nqb + i, q_col + h)),
        pl.BlockSpec((seq, qk_dim), lambda b, h, i: (b, k_col + h)),
        pl.BlockSpec((seq, LANES), lambda b, h, i: (b, v_col + h)),
    ]
    in_specs += [spec for _, spec in extras]
    row = pltpu.VMEM((1, nq), F32)
    return pl.pallas_call(
        kernel,
        grid=(batch, heads, nqb),
        in_specs=in_specs,
        out_specs=pl.BlockSpec((bq, LANES), lambda b, h, i: (b * nqb + i, h)),
        out_shape=jax.ShapeDtypeStruct((batch * seq, heads * LANES), BF16),
        scratch_shapes=[
            pltpu.VMEM((qk_dim, nq), BF16),
            pltpu.VMEM((LANES + ONES_ROWS, seq), BF16),
            pltpu.VMEM((bk // STRIP, bk, STRIP), F32),
            pltpu.VMEM((bk, nq), BF16), pltpu.VMEM((bk, nq), BF16),
            row, row, row, row,
            pltpu.VMEM((LANES + ONES_ROWS, nq), F32),
        ],
        compiler_params=_cparams(("parallel", "parallel", "arbitrary")),
        name=name,
    )(q, k, v, *[a for a, _ in extras])


def _gla_kernel(gq_ref, gk_ref, gv_ref, go_ref, gr_ref, gw_ref, gb_ref, gn_ref, o_ref, st_ref, *,
                block, chunk):
    c = chunk

    @pl.when(pl.program_id(1) == 0)
    def _():
        st_ref[...] = jnp.zeros(st_ref.shape, F32)

    logits = jnp.dot(gr_ref[...], gw_ref[...], preferred_element_type=F32) + gb_ref[...]
    soft = jnp.log2(1.0 + jnp.exp2(jnp.abs(logits) * (-LOG2E)))
    la = (jnp.minimum(logits, 0.0) * LOG2E - soft) * (1.0 / GLA_TAU)

    row = lax.broadcasted_iota(jnp.int32, la.shape, 0)
    in_chunk = jnp.bitwise_and(row, c - 1)
    b_all = la
    s = 1
    while s < c:
        b_all = b_all + jnp.where(in_chunk >= s, pltpu.roll(b_all, s, 0), 0.0)
        s *= 2

    lane = lax.broadcasted_iota(jnp.int32, (c, LANES), 1)
    first_head = lane < GLA_DK
    crow = lax.broadcasted_iota(jnp.int32, (c, GLA_HEADS * GLA_DK), 0)
    ri = lax.broadcasted_iota(jnp.int32, (c, c), 0)
    ci = lax.broadcasted_iota(jnp.int32, (c, c), 1)
    lvl = jnp.where(ri > ci, jnp.bitwise_xor(ri, ci), 0)
    diag = ri == ci
    n_levels = c.bit_length() - 1
    level_masks = [lax.shift_right_logical(lvl, bit) == 1 for bit in range(n_levels)]
    er = lax.broadcasted_iota(jnp.int32, (2 * GLA_DV, LANES), 0)
    ec = lax.broadcasted_iota(jnp.int32, (2 * GLA_DV, LANES), 1)
    bd_mask = (er < GLA_DV) == (ec < GLA_DK)
    gn = gn_ref[...]

    def head_scores(qt, kt):
        out = []
        for p in range(GLA_HEADS // 2):
            qp = qt[:, p * LANES:(p + 1) * LANES]
            kp = kt[:, p * LANES:(p + 1) * LANES].astype(BF16)
            for hh in range(2):
                qh = jnp.where(first_head if hh == 0 else jnp.logical_not(first_head), qp, 0.0)
                out.append(lax.dot_general(qh.astype(BF16), kp, (((1,), (1,)), ((), ())),
                                           preferred_element_type=F32))
        return out

    for sc in range(block // c):
        rows = slice(sc * c, (sc + 1) * c)
        b = b_all[rows]
        q = gq_ref[rows, :].astype(F32) * (GLA_DK ** -0.5)
        k = gk_ref[rows, :].astype(F32)
        v = gv_ref[rows, :]

        attn = [jnp.where(diag, m, 0.0) for m in head_scores(q, k)]
        ref_b = b
        for bit in range(n_levels):
            hs = 1 << bit
            if hs > 1:
                ref_b = jnp.where(jnp.bitwise_and(crow, hs - 1) < hs // 2, ref_b,
                                  pltpu.roll(ref_b, hs // 2, 0))
            qt = q * jnp.exp2(jnp.minimum(b - ref_b, 0.0))
            nxt = pltpu.roll(ref_b, c - hs, 0)
            kt = k * jnp.exp2(jnp.minimum(nxt - b, 0.0))
            attn = [jnp.where(level_masks[bit], m, a) for m, a in zip(head_scores(qt, kt), attn)]

        b_end = b[c - 1:c, :]
        qb = (q * jnp.exp2(b)).astype(BF16)
        kd = (k * jnp.exp2(b_end - b)).astype(BF16)

        for p in range(GLA_HEADS // 2):
            st = st_ref[p]
            inter = lax.dot_general(qb[:, p * LANES:(p + 1) * LANES], st.astype(BF16),
                                    (((1,), (1,)), ((), ())), preferred_element_type=F32)
            vp = v[:, p * 2 * GLA_DV:(p + 1) * 2 * GLA_DV]
            for hh in range(2):
                h = 2 * p + hh
                cols = slice(h * GLA_DV, (h + 1) * GLA_DV)
                intra = jnp.dot(attn[h].astype(BF16), vp[:, hh * GLA_DV:(hh + 1) * GLA_DV],
                                preferred_element_type=F32)
                out = inter[:, hh * GLA_DV:(hh + 1) * GLA_DV] + intra
                g = go_ref[rows, cols].astype(F32)
                o_ref[rows, cols] = (_rms(out, gn) * (g / (1.0 + jnp.exp(-g)))).astype(o_ref.dtype)
            upd = lax.dot_general(vp, kd[:, p * LANES:(p + 1) * LANES], (((0,), (0,)), ((), ())),
                                  preferred_element_type=F32)
            st_ref[p] = (st * jnp.exp2(b_end[:, p * LANES:(p + 1) * LANES])
                         + jnp.where(bd_mask, upd, 0.0))


def _gla(proj, gw, gb, gn, layer, *, batch, seq, block, chunk):
    nc = seq // block
    qk = GLA_HEADS * GLA_DK
    vd = GLA_HEADS * GLA_DV
    assert block % chunk == 0 and chunk & (chunk - 1) == 0
    return pl.pallas_call(
        functools.partial(_gla_kernel, block=block, chunk=chunk),
        grid=(batch, nc),
        in_specs=[
            pl.BlockSpec((block, qk), lambda b, c: (b * nc + c, COL_GQ // qk)),
            pl.BlockSpec((block, qk), lambda b, c: (b * nc + c, COL_GK // qk)),
            pl.BlockSpec((block, vd), lambda b, c: (b * nc + c, COL_GV // vd)),
            pl.BlockSpec((block, vd), lambda b, c: (b * nc + c, COL_GO // vd)),
            pl.BlockSpec((block, LANES), lambda b, c: (b * nc + c, COL_MKR // LANES)),
            pl.BlockSpec((None, LANES, qk), lambda b, c: (layer, 0, 0)),
            pl.BlockSpec((None, 1, qk), lambda b, c: (layer, 0, 0)),
            pl.BlockSpec((None, 1, GLA_DV), lambda b, c: (layer, 0, 0)),
        ],
        out_specs=pl.BlockSpec((block, vd), lambda b, c: (b * nc + c, 0)),
        out_shape=jax.ShapeDtypeStruct((batch * seq, vd), BF16),
        scratch_shapes=[pltpu.VMEM((GLA_HEADS // 2, 2 * GLA_DV, LANES), F32)],
        compiler_params=_cparams(("parallel", "arbitrary")),
        name="gla",
    )(proj, proj, proj, proj, proj, gw, gb, gn)


def _out_proj_kernel(x_ref, a_ref, b_ref, c_ref, w_ref, o_ref):
    na, nb = a_ref.shape[1], b_ref.shape[1]
    acc = jnp.dot(a_ref[...], w_ref[0:na, :], preferred_element_type=F32)
    acc += jnp.dot(b_ref[...], w_ref[na:na + nb, :], preferred_element_type=F32)
    acc += jnp.dot(c_ref[...], w_ref[na + nb:, :], preferred_element_type=F32)
    o_ref[...] = x_ref[...] + acc


def _out_proj(x, oa, ob, oc, w, layer, *, tm):
    t, d = x.shape

    def act(a):
        return pl.BlockSpec((tm, a.shape[1]), lambda i: (i, 0))

    return pl.pallas_call(
        _out_proj_kernel,
        grid=(t // tm,),
        in_specs=[act(x), act(oa), act(ob), act(oc),
                  pl.BlockSpec((None, w.shape[1], d), lambda i: (layer, 0, 0),
                               pipeline_mode=pl.Buffered(1))],
        out_specs=pl.BlockSpec((tm, d), lambda i: (i, 0)),
        out_shape=jax.ShapeDtypeStruct((t, d), F32),
        compiler_params=_cparams(("parallel",)),
        name="out_proj",
    )(x, oa, ob, oc, w)


def _ffn_kernel(x_ref, g_ref, wg_ref, wu_ref, wd_ref, fg_ref, o_ref, h_ref, *, final_norm):
    f = pl.program_id(1)

    def swiglu(h):
        gate = jnp.dot(h, wg_ref[...], preferred_element_type=F32)
        up = jnp.dot(h, wu_ref[...], preferred_element_type=F32)
        act = (gate / (1.0 + jnp.exp(-gate)) * up).astype(BF16)
        return jnp.dot(act, wd_ref[...], preferred_element_type=F32)

    @pl.when(f == 0)
    def _():
        for s in range(x_ref.shape[0] // FFN_SLAB):
            rows = slice(s * FFN_SLAB, (s + 1) * FFN_SLAB)
            x = x_ref[rows, :]
            h = _rms(x, g_ref[...]).astype(BF16)
            h_ref[rows, :] = h
            o_ref[rows, :] = x + swiglu(h)

    @pl.when(f > 0)
    def _():
        o_ref[...] += swiglu(h_ref[...])

    if final_norm:
        @pl.when(f == pl.num_programs(1) - 1)
        def _():
            o_ref[...] = _rms(o_ref[...], fg_ref[...])


def _ffn(x, g, wg, wu, wd, fg, layer, *, tm, tf, final_norm):
    t, d = x.shape
    ff = wg.shape[-1]
    return pl.pallas_call(
        functools.partial(_ffn_kernel, final_norm=final_norm),
        grid=(t // tm, ff // tf),
        in_specs=[
            pl.BlockSpec((tm, d), lambda i, f: (i, 0)),
            pl.BlockSpec((None, 1, d), lambda i, f: (layer, 0, 0)),
            pl.BlockSpec((None, d, tf), lambda i, f: (layer, 0, f)),
            pl.BlockSpec((None, d, tf), lambda i, f: (layer, 0, f)),
            pl.BlockSpec((None, tf, d), lambda i, f: (layer, f, 0)),
            pl.BlockSpec((1, d), lambda i, f: (0, 0)),
        ],
        out_specs=pl.BlockSpec((tm, d), lambda i, f: (i, 0)),
        out_shape=jax.ShapeDtypeStruct((t, d), F32),
        scratch_shapes=[pltpu.VMEM((tm, d), BF16)],
        compiler_params=_cparams(("parallel", "arbitrary")),
        name="ffn",
    )(x, g, wg, wu, wd, fg)


def _prep_w_in(w_in):
    sizes = (768, 768, 768, 256, 256, 512, 16, 512, 512, 256, 64)
    offs = [0]
    for s in sizes:
        offs.append(offs[-1] + s)
    d_q, d_k, d_v, g_q, g_k, g_v, g_r, g_o, m_qa, m_kva, m_kr = (
        w_in[..., offs[i]:offs[i + 1]] for i in range(len(sizes)))
    used = COL_GR + GLA_GATE_RANK
    pad = jnp.zeros(w_in.shape[:-1] + (PROJ_COLS - used,), w_in.dtype)
    parts = [d_q, d_k, d_v, g_q, g_k, m_kva, g_v, g_o, m_qa, m_kr, g_r, pad]
    return jnp.concatenate([p.astype(BF16) for p in parts], axis=-1)


def _prep_w_q_b(w):
    depth, r, _ = w.shape
    w = w.reshape(depth, r, MLA_HEADS, MLA_NOPE + MLA_ROPE)
    w = jnp.pad(w, ((0, 0), (0, 0), (0, 0), (0, MLA_QK - MLA_NOPE - MLA_ROPE)))
    return w.reshape(depth, r, MLA_HEADS * MLA_QK).astype(BF16)


def _prep_w_kv_b(w):
    depth, r, _ = w.shape
    w = w.reshape(depth, r, MLA_HEADS, MLA_NOPE + MLA_VD)
    k = w[..., :MLA_NOPE].reshape(depth, r, MLA_HEADS * MLA_NOPE)
    v = w[..., MLA_NOPE:].reshape(depth, r, MLA_HEADS * MLA_VD)
    return jnp.concatenate([k, v], axis=-1).astype(BF16)


def _prep_gate_w(w):
    depth = w.shape[0]
    lo = COL_GR - COL_MKR
    out = jnp.zeros((depth, LANES, w.shape[-1]), w.dtype)
    return out.at[:, lo:lo + GLA_GATE_RANK, :].set(w).astype(BF16)


def kernel(x, positions, attn_norm, w_in, diff_lambda_q1, diff_lambda_k1, diff_lambda_q2, diff_lambda_k2, diff_subln, gla_gate_w, gla_gate_b, gla_norm, mla_q_norm, mla_w_q_b, mla_kv_norm, mla_w_kv_b, w_out, ffn_norm, w_gate, w_up, w_down, final_norm):
    batch, seq, d = x.shape
    depth = w_in.shape[0]
    t = batch * seq
    assert d == D_MODEL and seq % max(TILES["diff_q"], TILES["mla_q"]) == 0

    w_in_p = _prep_w_in(w_in)
    wq_p = _prep_w_q_b(mla_w_q_b)
    wkv_p = _prep_w_kv_b(mla_w_kv_b)
    gw_p = _prep_gate_w(gla_gate_w)
    wo_p = w_out.astype(BF16)
    wg_p, wu_p, wd_p = w_gate.astype(BF16), w_up.astype(BF16), w_down.astype(BF16)
    lam_p = jnp.stack([diff_lambda_q1, diff_lambda_k1, diff_lambda_q2, diff_lambda_k2], axis=1).astype(F32)
    lam_init = jnp.asarray([0.8 - 0.6 * math.exp(-0.3 * l) for l in range(depth)], F32)
    cst = jnp.zeros((depth, 1, LANES), F32).at[:, 0, 0].set(lam_init)

    def row3(a):
        return a.astype(F32)[:, None, :]

    posb = jnp.broadcast_to(positions.astype(F32).reshape(t, 1), (t, LANES))
    half = MLA_ROPE // 2
    inv = 1.0 / (ROPE_THETA ** (jnp.arange(0, MLA_ROPE, 2, dtype=F32) / MLA_ROPE))
    inv = jnp.concatenate([inv, inv, jnp.zeros((LANES - 2 * half,), F32)])[None, :]
    cos, sa, sb = _rope_tables(posb, inv, tm=TILES["rope_rows"])

    xs = x.reshape(t, d)
    for l in range(depth):
        proj = _norm_matmul(xs, row3(attn_norm), w_in_p, l, tm=TILES["in_proj_rows"])
        mq, mk, mv = _mla_prep(proj, row3(mla_q_norm), row3(mla_kv_norm), wq_p, wkv_p, cos, sa, sb, l,
                               tm=TILES["mla_prep_rows"])
        diff_extras = [
            (lam_p, pl.BlockSpec((None, 4, DIFF_HD), lambda b, h, i, l=l: (l, 0, 0))),
            (cst, pl.BlockSpec((None, 1, LANES), lambda b, h, i, l=l: (l, 0, 0))),
            (row3(diff_subln), pl.BlockSpec((None, 1, DIFF_VD), lambda b, h, i, l=l: (l, 0, 0))),
        ]
        o_diff = _attention(proj, proj, proj, diff_extras, batch=batch, seq=seq, heads=DIFF_HEADS,
                            qk_dim=LANES, q_col=COL_DQ // LANES, k_col=COL_DK // LANES,
                            v_col=COL_DV // LANES, bq=TILES["diff_q"], bk=TILES["diff_k"], diff=True,
                            name="diff_attn")
        o_gla = _gla(proj, gw_p, row3(gla_gate_b), row3(gla_norm), l, batch=batch, seq=seq,
                     block=TILES["gla_block"], chunk=TILES["gla_chunk"])
        o_mla = _attention(mq, mk, mv, [], batch=batch, seq=seq, heads=MLA_HEADS,
                           qk_dim=MLA_QK, q_col=0, k_col=0, v_col=0, bq=TILES["mla_q"],
                           bk=TILES["mla_k"], diff=False,
                           name="mla_attn")
        x1 = _out_proj(xs, o_diff, o_gla, o_mla, wo_p, l, tm=TILES["out_proj_rows"])
        xs = _ffn(x1, row3(ffn_norm), wg_p, wu_p, wd_p, final_norm.astype(F32)[None, :], l,
                  tm=TILES["ffn_rows"], tf=TILES["ffn_cols"], final_norm=(l == depth - 1))
    return xs.reshape(batch, seq, d)
```

```python
import functools
import math

import jax
import jax.numpy as jnp
from jax import lax
from jax.experimental import pallas as pl
from jax.experimental.pallas import tpu as pltpu

F32 = jnp.float32
BF16 = jnp.bfloat16

D_MODEL = 2048
DIFF_HEADS = 6
DIFF_HD = 64
DIFF_VD = 128
GLA_HEADS = 4
GLA_DK = 64
GLA_DV = 128
GLA_GATE_RANK = 16
GLA_TAU = 16.0
MLA_HEADS = 6
MLA_Q_RANK = 512
MLA_KV_RANK = 256
MLA_NOPE = 128
MLA_ROPE = 64
MLA_VD = 128
ROPE_THETA = 10000.0
D_FF = 5632
EPS = 1e-6
LOG2E = 1.4426950408889634

LANES = 128
VMEM_LIMIT = 56 * 1024 * 1024

PROJ_COLS = 4736
COL_DQ, COL_DK, COL_DV = 0, 768, 1536
COL_GQ, COL_GK, COL_MKVA, COL_GV, COL_GO, COL_MQA, COL_MKR, COL_GR = (
    2304, 2560, 2816, 3072, 3584, 4096, 4608, 4672)
MLA_QK = 256
STRIP = 256
ONES_ROWS = 16
FFN_SLAB = 256
GAP_LIMIT = 64.0

TILES = dict(
    rope_rows=1024,
    in_proj_rows=512,
    mla_prep_rows=1024,
    diff_q=1024, diff_k=512,
    diff_unroll=4,
    mla_q=2048, mla_k=512, mla_unroll=2,
    gla_block=256, gla_chunk=128,
    out_proj_rows=1024,
    ffn_rows=1024, ffn_cols=512,
)


def _cparams(sem):
    return pltpu.CompilerParams(dimension_semantics=sem, vmem_limit_bytes=VMEM_LIMIT)


def _rms(x, g):
    return x * lax.rsqrt(jnp.mean(x * x, axis=-1, keepdims=True) + EPS) * g


def _norm_matmul_kernel(x_ref, g_ref, w_ref, o_ref):
    h = _rms(x_ref[...].astype(F32), g_ref[...]).astype(BF16)
    o_ref[...] = jnp.dot(h, w_ref[...], preferred_element_type=F32).astype(o_ref.dtype)


def _norm_matmul(x, g, w, layer, *, tm):
    t, d = x.shape
    n = w.shape[-1]
    return pl.pallas_call(
        _norm_matmul_kernel,
        grid=(t // tm,),
        in_specs=[
            pl.BlockSpec((tm, d), lambda i: (i, 0)),
            pl.BlockSpec((None, 1, d), lambda i: (layer, 0, 0)),
            pl.BlockSpec((None, d, n), lambda i: (layer, 0, 0), pipeline_mode=pl.Buffered(1)),
        ],
        out_specs=pl.BlockSpec((tm, n), lambda i: (i, 0)),
        out_shape=jax.ShapeDtypeStruct((t, n), BF16),
        compiler_params=_cparams(("parallel",)),
        name="in_proj",
    )(x, g, w)


def _rope_table_kernel(pos_ref, inv_ref, cos_ref, sa_ref, sb_ref):
    ang = pos_ref[...] * inv_ref[...]
    c = jnp.cos(ang)
    s = jnp.sin(ang)
    lane = lax.broadcasted_iota(jnp.int32, ang.shape, 1)
    cos_ref[...] = jnp.where(lane < MLA_ROPE, c, 0.0)
    sa_ref[...] = jnp.where(lane < MLA_ROPE // 2, -s, 0.0)
    sb_ref[...] = jnp.where(lane < MLA_ROPE // 2, 0.0, jnp.where(lane < MLA_ROPE, s, 0.0))


def _rope_tables(posb, inv, *, tm):
    t = posb.shape[0]
    spec = pl.BlockSpec((tm, LANES), lambda i: (i, 0))
    return pl.pallas_call(
        _rope_table_kernel,
        grid=(t // tm,),
        in_specs=[spec, pl.BlockSpec((1, LANES), lambda i: (0, 0))],
        out_specs=[spec, spec, spec],
        out_shape=[jax.ShapeDtypeStruct((t, LANES), F32)] * 3,
        compiler_params=_cparams(("parallel",)),
        name="rope_tables",
    )(posb, inv)


def _rope128(x, cos, sa, sb):
    return x * cos + pltpu.roll(x, 96, 1) * sa + pltpu.roll(x, 32, 1) * sb


def _mla_prep_kernel(qa_ref, kva_ref, kr_ref, gq_ref, gkv_ref, wq_ref, wkv_ref,
                     cos_ref, sa_ref, sb_ref, q_ref, k_ref, v_ref):
    cos, sa, sb = cos_ref[...], sa_ref[...], sb_ref[...]
    scale = (MLA_NOPE + MLA_ROPE) ** -0.5 * LOG2E

    hq = _rms(qa_ref[...].astype(F32), gq_ref[...]).astype(BF16)
    q = jnp.dot(hq, wq_ref[...], preferred_element_type=F32)
    for h in range(MLA_HEADS):
        lo = h * MLA_QK
        q_ref[:, lo:lo + LANES] = (q[:, lo:lo + LANES] * scale).astype(BF16)
        rp = _rope128(q[:, lo + LANES:lo + 2 * LANES], cos, sa, sb)
        q_ref[:, lo + LANES:lo + 2 * LANES] = (rp * scale).astype(BF16)

    hk = _rms(kva_ref[...].astype(F32), gkv_ref[...]).astype(BF16)
    kv = jnp.dot(hk, wkv_ref[...], preferred_element_type=F32)
    lane = lax.broadcasted_iota(jnp.int32, cos.shape, 1)
    kr = jnp.where(lane < MLA_ROPE, kr_ref[...].astype(F32), 0.0)
    krr = _rope128(kr, cos, sa, sb).astype(BF16)
    for h in range(MLA_HEADS):
        lo = h * MLA_QK
        k_ref[:, lo:lo + LANES] = kv[:, h * LANES:(h + 1) * LANES].astype(BF16)
        k_ref[:, lo + LANES:lo + 2 * LANES] = krr
    v_ref[...] = kv[:, MLA_HEADS * LANES:].astype(BF16)


def _mla_prep(proj, gq, gkv, wq, wkv, cos, sa, sb, layer, *, tm):
    t = proj.shape[0]
    nq = MLA_HEADS * MLA_QK
    nv = MLA_HEADS * MLA_VD
    tab = pl.BlockSpec((tm, LANES), lambda i: (i, 0))
    return pl.pallas_call(
        _mla_prep_kernel,
        grid=(t // tm,),
        in_specs=[
            pl.BlockSpec((tm, MLA_Q_RANK), lambda i: (i, COL_MQA // MLA_Q_RANK)),
            pl.BlockSpec((tm, MLA_KV_RANK), lambda i: (i, COL_MKVA // MLA_KV_RANK)),
            pl.BlockSpec((tm, LANES), lambda i: (i, COL_MKR // LANES)),
            pl.BlockSpec((None, 1, MLA_Q_RANK), lambda i: (layer, 0, 0)),
            pl.BlockSpec((None, 1, MLA_KV_RANK), lambda i: (layer, 0, 0)),
            pl.BlockSpec((None, MLA_Q_RANK, nq), lambda i: (layer, 0, 0)),
            pl.BlockSpec((None, MLA_KV_RANK, nq), lambda i: (layer, 0, 0)),
            tab, tab, tab,
        ],
        out_specs=[
            pl.BlockSpec((tm, nq), lambda i: (i, 0)),
            pl.BlockSpec((tm, nq), lambda i: (i, 0)),
            pl.BlockSpec((tm, nv), lambda i: (i, 0)),
        ],
        out_shape=[
            jax.ShapeDtypeStruct((t, nq), BF16),
            jax.ShapeDtypeStruct((t, nq), BF16),
            jax.ShapeDtypeStruct((t, nv), BF16),
        ],
        compiler_params=_cparams(("parallel",)),
        name="mla_prep",
    )(proj, proj, proj, gq, gkv, wq, wkv, cos, sa, sb)


def _attn_kernel(*refs, n_maps, bq, bk, diff, gap_limit, unroll):
    if diff:
        q_ref, k_ref, v_ref, lam_ref, cst_ref, subln_ref, o_ref = refs[:7]
    else:
        q_ref, k_ref, v_ref, o_ref = refs[:4]
    qt_s, vt_s, bias_s, p0, p1, al0, al1, r_s, over_s, acc_s = refs[-10:]
    pbuf, abuf = (p0, p1), (al0, al1)
    qi = pl.program_id(2)
    vd = LANES

    @pl.when(qi == 0)
    def _():
        def tr(c, carry):
            off = pl.multiple_of(c * bk, bk)
            vt_s[0:vd, pl.ds(off, bk)] = v_ref[pl.ds(off, bk), :].astype(F32).T.astype(BF16)
            return carry
        lax.fori_loop(0, vt_s.shape[1] // bk, tr, 0)
        vt_s[vd:, :] = jnp.ones((vt_s.shape[0] - vd, vt_s.shape[1]), BF16)
        key = lax.broadcasted_iota(jnp.int32, (bk, STRIP), 0)
        qry = lax.broadcasted_iota(jnp.int32, (bk, STRIP), 1)
        for t in range(bk // STRIP):
            bias_s[t] = jnp.where(key <= qry + t * STRIP, 0.0, -jnp.inf)

    nq = n_maps * bq
    strips = [slice(c * STRIP, (c + 1) * STRIP) for c in range(nq // STRIP)]
    groups = [[strips[g + m * (bq // STRIP)] for m in range(n_maps)] for g in range(bq // STRIP)]
    ratio = bq // bk
    first_diag = ratio * qi
    n_blocks = first_diag + ratio

    def load_queries(group):
        rows = slice(group[0].start, group[0].stop)
        if diff:
            qt = (q_ref[rows, :].astype(F32) * (DIFF_HD ** -0.5 * LOG2E)).T
            row = lax.broadcasted_iota(jnp.int32, qt.shape, 0)
            qt_s[:, group[0]] = jnp.where(row < DIFF_HD, qt, 0.0).astype(BF16)
            qt_s[:, group[1]] = jnp.where(row < DIFF_HD, 0.0, qt).astype(BF16)
        else:
            qt_s[:, group[0]] = q_ref[rows, :].astype(F32).T.astype(BF16)
        for cs in group:
            acc_s[:, cs] = jnp.zeros((acc_s.shape[0], STRIP), F32)

    def store_output(group):
        rows = slice(group[0].start, group[0].stop)
        ot = [acc_s[0:vd, cs] * (1.0 / acc_s[vd:vd + 1, cs]) for cs in group]
        if diff:
            lp = lam_ref[...]
            lam_init = cst_ref[:, 0:1]
            lam = (jnp.exp(jnp.sum(lp[0:1] * lp[1:2], axis=-1, keepdims=True))
                   - jnp.exp(jnp.sum(lp[2:3] * lp[3:4], axis=-1, keepdims=True)) + lam_init)
            d = (ot[0] - lam * ot[1]).T
            o_ref[rows, :] = (_rms(d, subln_ref[...]) * (1.0 - lam_init)).astype(o_ref.dtype)
        else:
            o_ref[rows, :] = ot[0].T.astype(o_ref.dtype)

    def scores(j, masked, cs, rows=bk):
        off = pl.multiple_of(j * bk, bk)
        s = jnp.dot(k_ref[pl.ds(off, rows), :], qt_s[:, cs], preferred_element_type=F32)
        if masked:
            key = lax.broadcasted_iota(jnp.int32, s.shape, 0) + (j * bk - qi * bq)
            qry = jnp.bitwise_and(lax.broadcasted_iota(jnp.int32, s.shape, 1) + cs.start, bq - 1)
            s = jnp.where(key <= qry, s, -jnp.inf)
        return s

    def first_reference(cs):
        mb = jnp.max(scores(0, True, cs, rows=ONES_ROWS), axis=0, keepdims=True)
        r_s[:, cs] = mb
        abuf[0][:, cs] = jnp.ones_like(mb)
        abuf[1][:, cs] = jnp.ones_like(mb)
        over_s[:, cs] = jnp.zeros_like(mb)

    def probs(j, slot, bias, cs):
        rows = bk if bias is None else (bias + 1) * STRIP
        s = scores(j, False, cs, rows=rows)
        if bias is not None:
            s = s + bias_s[bias, 0:rows, :]
        r_old = r_s[:, cs]
        pbuf[slot][0:rows, cs] = jnp.exp2(s - r_old).astype(BF16)
        mb = jnp.max(s, axis=0, keepdims=True)
        r_new = jnp.maximum(r_old, mb)
        abuf[1 - slot][:, cs] = jnp.exp2(r_old - r_new)
        over_s[:, cs] = jnp.maximum(over_s[:, cs], mb - r_old)
        r_s[:, cs] = r_new

    def values(j, slot, cs, rows=bk):
        off = pl.multiple_of(j * bk, bk)
        upd = jnp.dot(vt_s[:, pl.ds(off, rows)], pbuf[slot][0:rows, cs],
                      preferred_element_type=F32)
        acc_s[:, cs] = acc_s[:, cs] * abuf[slot][:, cs] + upd

    def pair(j):
        for cs in strips:
            values(j, 0, cs)
            probs(j + 1, 1, None, cs)
        for cs in strips:
            values(j + 1, 1, cs)
            probs(j + 2, 0, None, cs)

    def pairs(first, count):
        for i in range(count):
            pair(2 * (first + i))

    def main_body(u, carry):
        pairs(unroll * u, unroll)
        return carry

    def visibility(d, cs):
        lo = cs.start % bq
        if lo + STRIP <= d * bk:
            return "none"
        return "all" if lo >= (d + 1) * bk else "part"

    def diag_bias(d, cs):
        return (cs.start % bq - d * bk) // STRIP if visibility(d, cs) == "part" else None

    def diag_probs(d, cs):
        if visibility(d, cs) != "none":
            probs(first_diag + d, d % 2, diag_bias(d, cs), cs)

    def diag_values(d, cs):
        if visibility(d, cs) != "none":
            t = diag_bias(d, cs)
            values(first_diag + d, d % 2, cs, rows=bk if t is None else (t + 1) * STRIP)

    def first_block(diagonal):
        for group in groups:
            load_queries(group)
            for cs in group:
                first_reference(cs)
                probs(0, 0, diag_bias(0, cs) if diagonal else None, cs)

    @pl.when(qi == 0)
    def _():
        first_block(True)

    @pl.when(qi >= 1)
    def _():
        first_block(False)

    n_pairs = jnp.maximum(first_diag // 2 - 1, 0)
    n_main = n_pairs // unroll
    lax.fori_loop(0, n_main, main_body, 0)
    left = n_pairs - n_main * unroll

    if unroll == 4:
        @pl.when(left >= 2)
        def _():
            pairs(n_main * unroll, 2)

    odd_pair = jnp.bitwise_and(n_pairs, 1) == 1

    def diagonal_blocks():
        for d in range(1, ratio):
            for cs in strips:
                diag_values(d - 1, cs)
                diag_probs(d, cs)
        for cs in strips:
            diag_values(ratio - 1, cs)
        for group in groups:
            store_output(group)

    def last_blocks(with_pair):
        if with_pair:
            pair(2 * (n_pairs - 1))
        for cs in strips:
            values(first_diag - 2, 0, cs)
            probs(first_diag - 1, 1, None, cs)
        for cs in strips:
            values(first_diag - 1, 1, cs)
            diag_probs(0, cs)
        diagonal_blocks()

    @pl.when(jnp.logical_and(qi >= 1, odd_pair))
    def _():
        last_blocks(True)

    @pl.when(jnp.logical_and(qi >= 1, jnp.logical_not(odd_pair)))
    def _():
        last_blocks(False)

    @pl.when(qi == 0)
    def _():
        diagonal_blocks()

    @pl.when(jnp.max(over_s[...]) > gap_limit)
    def _():
        r_s[...] = jnp.full(r_s.shape, -jnp.inf, F32)
        acc_s[...] = jnp.zeros(acc_s.shape, F32)

        def exact(i, carry):
            j = i // len(strips)
            lane0 = pl.multiple_of((i % len(strips)) * STRIP, STRIP)
            cs = pl.ds(lane0, STRIP)
            off = pl.multiple_of(j * bk, bk)
            s = jnp.dot(k_ref[pl.ds(off, bk), :], qt_s[:, cs], preferred_element_type=F32)
            key = lax.broadcasted_iota(jnp.int32, s.shape, 0) + (j * bk - qi * bq)
            qry = jnp.bitwise_and(lax.broadcasted_iota(jnp.int32, s.shape, 1) + lane0, bq - 1)
            s = jnp.where(key <= qry, s, -jnp.inf)
            m_old = r_s[:, cs]
            m_new = jnp.maximum(m_old, jnp.max(s, axis=0, keepdims=True))
            p = jnp.exp2(s - m_new).astype(BF16)
            upd = jnp.dot(vt_s[:, pl.ds(off, bk)], p, preferred_element_type=F32)
            acc_s[:, cs] = acc_s[:, cs] * jnp.exp2(m_old - m_new) + upd
            r_s[:, cs] = m_new
            return carry

        lax.fori_loop(0, n_blocks * len(strips), exact, 0)
        for group in groups:
            store_output(group)


def _attention(q, k, v, extras, *, batch, seq, heads, qk_dim, q_col, k_col, v_col, bq, bk, unroll, diff,
               name, gap_limit=GAP_LIMIT):
    n_maps = 2 if diff else 1
    nq = n_maps * bq
    nqb = seq // bq
    assert bq % (2 * bk) == 0 and bk % STRIP == 0 and seq % bq == 0 and unroll in (2, 4)
    kernel = functools.partial(_attn_kernel, n_maps=n_maps, bq=bq, bk=bk, diff=diff, gap_limit=gap_limit,
                               unroll=unroll)
    in_specs = [
        pl.BlockSpec((bq, qk_dim), lambda b, h, i: (b * nqb + i, q_col + h)),
        pl.BlockSpec((seq, qk_dim), lambda b, h, i: (b, k_col + h)),
        pl.BlockSpec((seq, LANES), lambda b, h, i: (b, v_col + h)),
    ]
    in_specs += [spec for _, spec in extras]
    row = pltpu.VMEM((1, nq), F32)
    return pl.pallas_call(
        kernel,
        grid=(batch, heads, nqb),
        in_specs=in_specs,
        out_specs=pl.BlockSpec((bq, LANES), lambda b, h, i: (b * nqb + i, h)),
        out_shape=jax.ShapeDtypeStruct((batch * seq, heads * LANES), BF16),
        scratch_shapes=[
            pltpu.VMEM((qk_dim, nq), BF16),
            pltpu.VMEM((LANES + ONES_ROWS, seq), BF16),
            pltpu.VMEM((bk // STRIP, bk, STRIP), F32),
            pltpu.VMEM((bk, nq), BF16), pltpu.VMEM((bk, nq), BF16),
            row, row, row, row,
            pltpu.VMEM((LANES + ONES_ROWS, nq), F32),
        ],
        compiler_params=_cparams(("parallel", "parallel", "arbitrary")),
        name=name,
    )(q, k, v, *[a for a, _ in extras])


def _gla_kernel(gq_ref, gk_ref, gv_ref, go_ref, gr_ref, gw_ref, gb_ref, gn_ref, o_ref, st_ref, *,
                block, chunk):
    c = chunk

    @pl.when(pl.program_id(1) == 0)
    def _():
        st_ref[...] = jnp.zeros(st_ref.shape, F32)

    logits = jnp.dot(gr_ref[...], gw_ref[...], preferred_element_type=F32) + gb_ref[...]
    soft = jnp.log2(1.0 + jnp.exp2(jnp.abs(logits) * (-LOG2E)))
    la = (jnp.minimum(logits, 0.0) * LOG2E - soft) * (1.0 / GLA_TAU)

    row = lax.broadcasted_iota(jnp.int32, la.shape, 0)
    in_chunk = jnp.bitwise_and(row, c - 1)
    b_all = la
    s = 1
    while s < c:
        b_all = b_all + jnp.where(in_chunk >= s, pltpu.roll(b_all, s, 0), 0.0)
        s *= 2

    lane = lax.broadcasted_iota(jnp.int32, (c, LANES), 1)
    first_head = lane < GLA_DK
    crow = lax.broadcasted_iota(jnp.int32, (c, GLA_HEADS * GLA_DK), 0)
    ri = lax.broadcasted_iota(jnp.int32, (c, c), 0)
    ci = lax.broadcasted_iota(jnp.int32, (c, c), 1)
    lvl = jnp.where(ri > ci, jnp.bitwise_xor(ri, ci), 0)
    diag = ri == ci
    n_levels = c.bit_length() - 1
    level_masks = [lax.shift_right_logical(lvl, bit) == 1 for bit in range(n_levels)]
    er = lax.broadcasted_iota(jnp.int32, (2 * GLA_DV, LANES), 0)
    ec = lax.broadcasted_iota(jnp.int32, (2 * GLA_DV, LANES), 1)
    bd_mask = (er < GLA_DV) == (ec < GLA_DK)
    gn = gn_ref[...]

    def head_scores(qt, kt):
        out = []
        for p in range(GLA_HEADS // 2):
            qp = qt[:, p * LANES:(p + 1) * LANES]
            kp = kt[:, p * LANES:(p + 1) * LANES].astype(BF16)
            for hh in range(2):
                qh = jnp.where(first_head if hh == 0 else jnp.logical_not(first_head), qp, 0.0)
                out.append(lax.dot_general(qh.astype(BF16), kp, (((1,), (1,)), ((), ())),
                                           preferred_element_type=F32))
        return out

    for sc in range(block // c):
        rows = slice(sc * c, (sc + 1) * c)
        b = b_all[rows]
        q = gq_ref[rows, :].astype(F32) * (GLA_DK ** -0.5)
        k = gk_ref[rows, :].astype(F32)
        v = gv_ref[rows, :]

        attn = [jnp.where(diag, m, 0.0) for m in head_scores(q, k)]
        ref_b = b
        for bit in range(n_levels):
            hs = 1 << bit
            if hs > 1:
                ref_b = jnp.where(jnp.bitwise_and(crow, hs - 1) < hs // 2, ref_b,
                                  pltpu.roll(ref_b, hs // 2, 0))
            qt = q * jnp.exp2(jnp.minimum(b - ref_b, 0.0))
            nxt = pltpu.roll(ref_b, c - hs, 0)
            kt = k * jnp.exp2(jnp.minimum(nxt - b, 0.0))
            attn = [jnp.where(level_masks[bit], m, a) for m, a in zip(head_scores(qt, kt), attn)]

        b_end = b[c - 1:c, :]
        qb = (q * jnp.exp2(b)).astype(BF16)
        kd = (k * jnp.exp2(b_end - b)).astype(BF16)

        for p in range(GLA_HEADS // 2):
            st = st_ref[p]
            inter = lax.dot_general(qb[:, p * LANES:(p + 1) * LANES], st.astype(BF16),
                                    (((1,), (1,)), ((), ())), preferred_element_type=F32)
            vp = v[:, p * 2 * GLA_DV:(p + 1) * 2 * GLA_DV]
            for hh in range(2):
                h = 2 * p + hh
                cols = slice(h * GLA_DV, (h + 1) * GLA_DV)
                intra = jnp.dot(attn[h].astype(BF16), vp[:, hh * GLA_DV:(hh + 1) * GLA_DV],
                                preferred_element_type=F32)
                out = inter[:, hh * GLA_DV:(hh + 1) * GLA_DV] + intra
                g = go_ref[rows, cols].astype(F32)
                o_ref[rows, cols] = (_rms(out, gn) * (g / (1.0 + jnp.exp(-g)))).astype(o_ref.dtype)
            upd = lax.dot_general(vp, kd[:, p * LANES:(p + 1) * LANES], (((0,), (0,)), ((), ())),
                                  preferred_element_type=F32)
            st_ref[p] = (st * jnp.exp2(b_end[:, p * LANES:(p + 1) * LANES])
                         + jnp.where(bd_mask, upd, 0.0))


def _gla(proj, gw, gb, gn, layer, *, batch, seq, block, chunk):
    nc = seq // block
    qk = GLA_HEADS * GLA_DK
    vd = GLA_HEADS * GLA_DV
    assert block % chunk == 0 and chunk & (chunk - 1) == 0
    return pl.pallas_call(
        functools.partial(_gla_kernel, block=block, chunk=chunk),
        grid=(batch, nc),
        in_specs=[
            pl.BlockSpec((block, qk), lambda b, c: (b * nc + c, COL_GQ // qk)),
            pl.BlockSpec((block, qk), lambda b, c: (b * nc + c, COL_GK // qk)),
            pl.BlockSpec((block, vd), lambda b, c: (b * nc + c, COL_GV // vd)),
            pl.BlockSpec((block, vd), lambda b, c: (b * nc + c, COL_GO // vd)),
            pl.BlockSpec((block, LANES), lambda b, c: (b * nc + c, COL_MKR // LANES)),
            pl.BlockSpec((None, LANES, qk), lambda b, c: (layer, 0, 0)),
            pl.BlockSpec((None, 1, qk), lambda b, c: (layer, 0, 0)),
            pl.BlockSpec((None, 1, GLA_DV), lambda b, c: (layer, 0, 0)),
        ],
        out_specs=pl.BlockSpec((block, vd), lambda b, c: (b * nc + c, 0)),
        out_shape=jax.ShapeDtypeStruct((batch * seq, vd), BF16),
        scratch_shapes=[pltpu.VMEM((GLA_HEADS // 2, 2 * GLA_DV, LANES), F32)],
        compiler_params=_cparams(("parallel", "arbitrary")),
        name="gla",
    )(proj, proj, proj, proj, proj, gw, gb, gn)


def _out_proj_kernel(x_ref, a_ref, b_ref, c_ref, w_ref, o_ref):
    na, nb = a_ref.shape[1], b_ref.shape[1]
    acc = jnp.dot(a_ref[...], w_ref[0:na, :], preferred_element_type=F32)
    acc += jnp.dot(b_ref[...], w_ref[na:na + nb, :], preferred_element_type=F32)
    acc += jnp.dot(c_ref[...], w_ref[na + nb:, :], preferred_element_type=F32)
    o_ref[...] = x_ref[...] + acc


def _out_proj(x, oa, ob, oc, w, layer, *, tm):
    t, d = x.shape

    def act(a):
        return pl.BlockSpec((tm, a.shape[1]), lambda i: (i, 0))

    return pl.pallas_call(
        _out_proj_kernel,
        grid=(t // tm,),
        in_specs=[act(x), act(oa), act(ob), act(oc),
                  pl.BlockSpec((None, w.shape[1], d), lambda i: (layer, 0, 0),
                               pipeline_mode=pl.Buffered(1))],
        out_specs=pl.BlockSpec((tm, d), lambda i: (i, 0)),
        out_shape=jax.ShapeDtypeStruct((t, d), F32),
        compiler_params=_cparams(("parallel",)),
        name="out_proj",
    )(x, oa, ob, oc, w)


def _ffn_kernel(x_ref, g_ref, wg_ref, wu_ref, wd_ref, fg_ref, o_ref, h_ref, *, final_norm):
    f = pl.program_id(1)

    def swiglu(h):
        gate = jnp.dot(h, wg_ref[...], preferred_element_type=F32)
        up = jnp.dot(h, wu_ref[...], preferred_element_type=F32)
        act = (gate / (1.0 + jnp.exp(-gate)) * up).astype(BF16)
        return jnp.dot(act, wd_ref[...], preferred_element_type=F32)

    @pl.when(f == 0)
    def _():
        for s in range(x_ref.shape[0] // FFN_SLAB):
            rows = slice(s * FFN_SLAB, (s + 1) * FFN_SLAB)
            x = x_ref[rows, :]
            h = _rms(x, g_ref[...]).astype(BF16)
            h_ref[rows, :] = h
            o_ref[rows, :] = x + swiglu(h)

    @pl.when(f > 0)
    def _():
        o_ref[...] += swiglu(h_ref[...])

    if final_norm:
        @pl.when(f == pl.num_programs(1) - 1)
        def _():
            o_ref[...] = _rms(o_ref[...], fg_ref[...])


def _ffn(x, g, wg, wu, wd, fg, layer, *, tm, tf, final_norm):
    t, d = x.shape
    ff = wg.shape[-1]
    return pl.pallas_call(
        functools.partial(_ffn_kernel, final_norm=final_norm),
        grid=(t // tm, ff // tf),
        in_specs=[
            pl.BlockSpec((tm, d), lambda i, f: (i, 0)),
            pl.BlockSpec((None, 1, d), lambda i, f: (layer, 0, 0)),
            pl.BlockSpec((None, d, tf), lambda i, f: (layer, 0, f)),
            pl.BlockSpec((None, d, tf), lambda i, f: (layer, 0, f)),
            pl.BlockSpec((None, tf, d), lambda i, f: (layer, f, 0)),
            pl.BlockSpec((1, d), lambda i, f: (0, 0)),
        ],
        out_specs=pl.BlockSpec((tm, d), lambda i, f: (i, 0)),
        out_shape=jax.ShapeDtypeStruct((t, d), F32),
        scratch_shapes=[pltpu.VMEM((tm, d), BF16)],
        compiler_params=_cparams(("parallel", "arbitrary")),
        name="ffn",
    )(x, g, wg, wu, wd, fg)


def _prep_w_in(w_in):
    sizes = (768, 768, 768, 256, 256, 512, 16, 512, 512, 256, 64)
    offs = [0]
    for s in sizes:
        offs.append(offs[-1] + s)
    d_q, d_k, d_v, g_q, g_k, g_v, g_r, g_o, m_qa, m_kva, m_kr = (
        w_in[..., offs[i]:offs[i + 1]] for i in range(len(sizes)))
    used = COL_GR + GLA_GATE_RANK
    pad = jnp.zeros(w_in.shape[:-1] + (PROJ_COLS - used,), w_in.dtype)
    parts = [d_q, d_k, d_v, g_q, g_k, m_kva, g_v, g_o, m_qa, m_kr, g_r, pad]
    return jnp.concatenate([p.astype(BF16) for p in parts], axis=-1)


def _prep_w_q_b(w):
    depth, r, _ = w.shape
    w = w.reshape(depth, r, MLA_HEADS, MLA_NOPE + MLA_ROPE)
    w = jnp.pad(w, ((0, 0), (0, 0), (0, 0), (0, MLA_QK - MLA_NOPE - MLA_ROPE)))
    return w.reshape(depth, r, MLA_HEADS * MLA_QK).astype(BF16)


def _prep_w_kv_b(w):
    depth, r, _ = w.shape
    w = w.reshape(depth, r, MLA_HEADS, MLA_NOPE + MLA_VD)
    k = w[..., :MLA_NOPE].reshape(depth, r, MLA_HEADS * MLA_NOPE)
    v = w[..., MLA_NOPE:].reshape(depth, r, MLA_HEADS * MLA_VD)
    return jnp.concatenate([k, v], axis=-1).astype(BF16)


def _prep_gate_w(w):
    depth = w.shape[0]
    lo = COL_GR - COL_MKR
    out = jnp.zeros((depth, LANES, w.shape[-1]), w.dtype)
    return out.at[:, lo:lo + GLA_GATE_RANK, :].set(w).astype(BF16)


def kernel(x, positions, attn_norm, w_in, diff_lambda_q1, diff_lambda_k1, diff_lambda_q2, diff_lambda_k2, diff_subln, gla_gate_w, gla_gate_b, gla_norm, mla_q_norm, mla_w_q_b, mla_kv_norm, mla_w_kv_b, w_out, ffn_norm, w_gate, w_up, w_down, final_norm):
    batch, seq, d = x.shape
    depth = w_in.shape[0]
    t = batch * seq
    assert d == D_MODEL and seq % max(TILES["diff_q"], TILES["mla_q"]) == 0

    w_in_p = _prep_w_in(w_in)
    wq_p = _prep_w_q_b(mla_w_q_b)
    wkv_p = _prep_w_kv_b(mla_w_kv_b)
    gw_p = _prep_gate_w(gla_gate_w)
    wo_p = w_out.astype(BF16)
    wg_p, wu_p, wd_p = w_gate.astype(BF16), w_up.astype(BF16), w_down.astype(BF16)
    lam_p = jnp.stack([diff_lambda_q1, diff_lambda_k1, diff_lambda_q2, diff_lambda_k2], axis=1).astype(F32)
    lam_init = jnp.asarray([0.8 - 0.6 * math.exp(-0.3 * l) for l in range(depth)], F32)
    cst = jnp.zeros((depth, 1, LANES), F32).at[:, 0, 0].set(lam_init)

    def row3(a):
        return a.astype(F32)[:, None, :]

    posb = jnp.broadcast_to(positions.astype(F32).reshape(t, 1), (t, LANES))
    half = MLA_ROPE // 2
    inv = 1.0 / (ROPE_THETA ** (jnp.arange(0, MLA_ROPE, 2, dtype=F32) / MLA_ROPE))
    inv = jnp.concatenate([inv, inv, jnp.zeros((LANES - 2 * half,), F32)])[None, :]
    cos, sa, sb = _rope_tables(posb, inv, tm=TILES["rope_rows"])

    xs = x.reshape(t, d)
    for l in range(depth):
        proj = _norm_matmul(xs, row3(attn_norm), w_in_p, l, tm=TILES["in_proj_rows"])
        mq, mk, mv = _mla_prep(proj, row3(mla_q_norm), row3(mla_kv_norm), wq_p, wkv_p, cos, sa, sb, l,
                               tm=TILES["mla_prep_rows"])
        diff_extras = [
            (lam_p, pl.BlockSpec((None, 4, DIFF_HD), lambda b, h, i, l=l: (l, 0, 0))),
            (cst, pl.BlockSpec((None, 1, LANES), lambda b, h, i, l=l: (l, 0, 0))),
            (row3(diff_subln), pl.BlockSpec((None, 1, DIFF_VD), lambda b, h, i, l=l: (l, 0, 0))),
        ]
        o_diff = _attention(proj, proj, proj, diff_extras, batch=batch, seq=seq, heads=DIFF_HEADS,
                            qk_dim=LANES, q_col=COL_DQ // LANES, k_col=COL_DK // LANES,
                            v_col=COL_DV // LANES, bq=TILES["diff_q"], bk=TILES["diff_k"],
                            unroll=TILES["diff_unroll"], diff=True, name="diff_attn")
        o_gla = _gla(proj, gw_p, row3(gla_gate_b), row3(gla_norm), l, batch=batch, seq=seq,
                     block=TILES["gla_block"], chunk=TILES["gla_chunk"])
        o_mla = _attention(mq, mk, mv, [], batch=batch, seq=seq, heads=MLA_HEADS,
                           qk_dim=MLA_QK, q_col=0, k_col=0, v_col=0, bq=TILES["mla_q"],
                           bk=TILES["mla_k"], unroll=TILES["mla_unroll"], diff=False, name="mla_attn")
        x1 = _out_proj(xs, o_diff, o_gla, o_mla, wo_p, l, tm=TILES["out_proj_rows"])
        xs = _ffn(x1, row3(ffn_norm), wg_p, wu_p, wd_p, final_norm.astype(F32)[None, :], l,
                  tm=TILES["ffn_rows"], tf=TILES["ffn_cols"], final_norm=(l == depth - 1))
    return xs.reshape(batch, seq, d)
```

```python
import functools
import math

import jax
import jax.numpy as jnp
from jax import lax
from jax.experimental import pallas as pl
from jax.experimental.pallas import tpu as pltpu

F32 = jnp.float32
BF16 = jnp.bfloat16

D_MODEL = 2048
DIFF_HEADS = 6
DIFF_HD = 64
DIFF_VD = 128
GLA_HEADS = 4
GLA_DK = 64
GLA_DV = 128
GLA_GATE_RANK = 16
GLA_TAU = 16.0
MLA_HEADS = 6
MLA_Q_RANK = 512
MLA_KV_RANK = 256
MLA_NOPE = 128
MLA_ROPE = 64
MLA_VD = 128
ROPE_THETA = 10000.0
D_FF = 5632
EPS = 1e-6
LOG2E = 1.4426950408889634

LANES = 128
VMEM_LIMIT = 56 * 1024 * 1024

PROJ_COLS = 4736
COL_DQ, COL_DK, COL_DV = 0, 768, 1536
COL_GQ, COL_GK, COL_MKVA, COL_GV, COL_GO, COL_MQA, COL_MKR, COL_GR = (
    2304, 2560, 2816, 3072, 3584, 4096, 4608, 4672)
MLA_QK = 256
STRIP = 256
ONES_ROWS = 16
FFN_SLAB = 256
GAP_LIMIT = 64.0

TILES = dict(
    rope_rows=1024,
    in_proj_rows=512,
    mla_prep_rows=1024,
    diff_q=1024, diff_k=512,
    diff_unroll=4,
    mla_q=2048, mla_k=512, mla_unroll=4,
    gla_block=256, gla_chunk=128,
    out_proj_rows=1024,
    ffn_rows=1024, ffn_cols=512,
)


def _cparams(sem):
    return pltpu.CompilerParams(dimension_semantics=sem, vmem_limit_bytes=VMEM_LIMIT)


def _rms(x, g):
    return x * lax.rsqrt(jnp.mean(x * x, axis=-1, keepdims=True) + EPS) * g


def _norm_matmul_kernel(x_ref, g_ref, w_ref, o_ref):
    h = _rms(x_ref[...].astype(F32), g_ref[...]).astype(BF16)
    o_ref[...] = jnp.dot(h, w_ref[...], preferred_element_type=F32).astype(o_ref.dtype)


def _norm_matmul(x, g, w, layer, *, tm):
    t, d = x.shape
    n = w.shape[-1]
    return pl.pallas_call(
        _norm_matmul_kernel,
        grid=(t // tm,),
        in_specs=[
            pl.BlockSpec((tm, d), lambda i: (i, 0)),
            pl.BlockSpec((None, 1, d), lambda i: (layer, 0, 0)),
            pl.BlockSpec((None, d, n), lambda i: (layer, 0, 0), pipeline_mode=pl.Buffered(1)),
        ],
        out_specs=pl.BlockSpec((tm, n), lambda i: (i, 0)),
        out_shape=jax.ShapeDtypeStruct((t, n), BF16),
        compiler_params=_cparams(("parallel",)),
        name="in_proj",
    )(x, g, w)


def _rope_table_kernel(pos_ref, inv_ref, cos_ref, sa_ref, sb_ref):
    ang = pos_ref[...] * inv_ref[...]
    c = jnp.cos(ang)
    s = jnp.sin(ang)
    lane = lax.broadcasted_iota(jnp.int32, ang.shape, 1)
    cos_ref[...] = jnp.where(lane < MLA_ROPE, c, 0.0)
    sa_ref[...] = jnp.where(lane < MLA_ROPE // 2, -s, 0.0)
    sb_ref[...] = jnp.where(lane < MLA_ROPE // 2, 0.0, jnp.where(lane < MLA_ROPE, s, 0.0))


def _rope_tables(posb, inv, *, tm):
    t = posb.shape[0]
    spec = pl.BlockSpec((tm, LANES), lambda i: (i, 0))
    return pl.pallas_call(
        _rope_table_kernel,
        grid=(t // tm,),
        in_specs=[spec, pl.BlockSpec((1, LANES), lambda i: (0, 0))],
        out_specs=[spec, spec, spec],
        out_shape=[jax.ShapeDtypeStruct((t, LANES), F32)] * 3,
        compiler_params=_cparams(("parallel",)),
        name="rope_tables",
    )(posb, inv)


def _rope128(x, cos, sa, sb):
    return x * cos + pltpu.roll(x, 96, 1) * sa + pltpu.roll(x, 32, 1) * sb


def _mla_prep_kernel(qa_ref, kva_ref, kr_ref, gq_ref, gkv_ref, wq_ref, wkv_ref,
                     cos_ref, sa_ref, sb_ref, q_ref, k_ref, v_ref):
    cos, sa, sb = cos_ref[...], sa_ref[...], sb_ref[...]
    scale = (MLA_NOPE + MLA_ROPE) ** -0.5 * LOG2E

    hq = _rms(qa_ref[...].astype(F32), gq_ref[...]).astype(BF16)
    q = jnp.dot(hq, wq_ref[...], preferred_element_type=F32)
    for h in range(MLA_HEADS):
        lo = h * MLA_QK
        q_ref[:, lo:lo + LANES] = (q[:, lo:lo + LANES] * scale).astype(BF16)
        rp = _rope128(q[:, lo + LANES:lo + 2 * LANES], cos, sa, sb)
        q_ref[:, lo + LANES:lo + 2 * LANES] = (rp * scale).astype(BF16)

    hk = _rms(kva_ref[...].astype(F32), gkv_ref[...]).astype(BF16)
    kv = jnp.dot(hk, wkv_ref[...], preferred_element_type=F32)
    lane = lax.broadcasted_iota(jnp.int32, cos.shape, 1)
    kr = jnp.where(lane < MLA_ROPE, kr_ref[...].astype(F32), 0.0)
    krr = _rope128(kr, cos, sa, sb).astype(BF16)
    for h in range(MLA_HEADS):
        lo = h * MLA_QK
        k_ref[:, lo:lo + LANES] = kv[:, h * LANES:(h + 1) * LANES].astype(BF16)
        k_ref[:, lo + LANES:lo + 2 * LANES] = krr
    v_ref[...] = kv[:, MLA_HEADS * LANES:].astype(BF16)


def _mla_prep(proj, gq, gkv, wq, wkv, cos, sa, sb, layer, *, tm):
    t = proj.shape[0]
    nq = MLA_HEADS * MLA_QK
    nv = MLA_HEADS * MLA_VD
    tab = pl.BlockSpec((tm, LANES), lambda i: (i, 0))
    return pl.pallas_call(
        _mla_prep_kernel,
        grid=(t // tm,),
        in_specs=[
            pl.BlockSpec((tm, MLA_Q_RANK), lambda i: (i, COL_MQA // MLA_Q_RANK)),
            pl.BlockSpec((tm, MLA_KV_RANK), lambda i: (i, COL_MKVA // MLA_KV_RANK)),
            pl.BlockSpec((tm, LANES), lambda i: (i, COL_MKR // LANES)),
            pl.BlockSpec((None, 1, MLA_Q_RANK), lambda i: (layer, 0, 0)),
            pl.BlockSpec((None, 1, MLA_KV_RANK), lambda i: (layer, 0, 0)),
            pl.BlockSpec((None, MLA_Q_RANK, nq), lambda i: (layer, 0, 0)),
            pl.BlockSpec((None, MLA_KV_RANK, nq), lambda i: (layer, 0, 0)),
            tab, tab, tab,
        ],
        out_specs=[
            pl.BlockSpec((tm, nq), lambda i: (i, 0)),
            pl.BlockSpec((tm, nq), lambda i: (i, 0)),
            pl.BlockSpec((tm, nv), lambda i: (i, 0)),
        ],
        out_shape=[
            jax.ShapeDtypeStruct((t, nq), BF16),
            jax.ShapeDtypeStruct((t, nq), BF16),
            jax.ShapeDtypeStruct((t, nv), BF16),
        ],
        compiler_params=_cparams(("parallel",)),
        name="mla_prep",
    )(proj, proj, proj, gq, gkv, wq, wkv, cos, sa, sb)


def _attn_kernel(*refs, n_maps, bq, bk, diff, gap_limit, unroll):
    if diff:
        q_ref, k_ref, v_ref, lam_ref, cst_ref, subln_ref, o_ref = refs[:7]
    else:
        q_ref, k_ref, v_ref, o_ref = refs[:4]
    qt_s, vt_s, bias_s, p0, p1, al0, al1, r_s, over_s, acc_s = refs[-10:]
    pbuf, abuf = (p0, p1), (al0, al1)
    qi = pl.program_id(2)
    vd = LANES

    @pl.when(qi == 0)
    def _():
        def tr(c, carry):
            off = pl.multiple_of(c * bk, bk)
            vt_s[0:vd, pl.ds(off, bk)] = v_ref[pl.ds(off, bk), :].astype(F32).T.astype(BF16)
            return carry
        lax.fori_loop(0, vt_s.shape[1] // bk, tr, 0)
        vt_s[vd:, :] = jnp.ones((vt_s.shape[0] - vd, vt_s.shape[1]), BF16)
        key = lax.broadcasted_iota(jnp.int32, (bk, STRIP), 0)
        qry = lax.broadcasted_iota(jnp.int32, (bk, STRIP), 1)
        for t in range(bk // STRIP):
            bias_s[t] = jnp.where(key <= qry + t * STRIP, 0.0, -jnp.inf)

    nq = n_maps * bq
    strips = [slice(c * STRIP, (c + 1) * STRIP) for c in range(nq // STRIP)]
    groups = [[strips[g + m * (bq // STRIP)] for m in range(n_maps)] for g in range(bq // STRIP)]
    ratio = bq // bk
    first_diag = ratio * qi
    n_blocks = first_diag + ratio

    def load_queries(group):
        rows = slice(group[0].start, group[0].stop)
        if diff:
            qt = (q_ref[rows, :].astype(F32) * (DIFF_HD ** -0.5 * LOG2E)).T
            row = lax.broadcasted_iota(jnp.int32, qt.shape, 0)
            qt_s[:, group[0]] = jnp.where(row < DIFF_HD, qt, 0.0).astype(BF16)
            qt_s[:, group[1]] = jnp.where(row < DIFF_HD, 0.0, qt).astype(BF16)
        else:
            qt_s[:, group[0]] = q_ref[rows, :].astype(F32).T.astype(BF16)
        for cs in group:
            acc_s[:, cs] = jnp.zeros((acc_s.shape[0], STRIP), F32)

    def store_output(group):
        rows = slice(group[0].start, group[0].stop)
        ot = [acc_s[0:vd, cs] * (1.0 / acc_s[vd:vd + 1, cs]) for cs in group]
        if diff:
            lp = lam_ref[...]
            lam_init = cst_ref[:, 0:1]
            lam = (jnp.exp(jnp.sum(lp[0:1] * lp[1:2], axis=-1, keepdims=True))
                   - jnp.exp(jnp.sum(lp[2:3] * lp[3:4], axis=-1, keepdims=True)) + lam_init)
            d = (ot[0] - lam * ot[1]).T
            o_ref[rows, :] = (_rms(d, subln_ref[...]) * (1.0 - lam_init)).astype(o_ref.dtype)
        else:
            o_ref[rows, :] = ot[0].T.astype(o_ref.dtype)

    def scores(j, masked, cs, rows=bk):
        off = pl.multiple_of(j * bk, bk)
        s = jnp.dot(k_ref[pl.ds(off, rows), :], qt_s[:, cs], preferred_element_type=F32)
        if masked:
            key = lax.broadcasted_iota(jnp.int32, s.shape, 0) + (j * bk - qi * bq)
            qry = jnp.bitwise_and(lax.broadcasted_iota(jnp.int32, s.shape, 1) + cs.start, bq - 1)
            s = jnp.where(key <= qry, s, -jnp.inf)
        return s

    def first_reference(cs):
        mb = jnp.max(scores(0, True, cs, rows=ONES_ROWS), axis=0, keepdims=True)
        r_s[:, cs] = mb
        abuf[0][:, cs] = jnp.ones_like(mb)
        abuf[1][:, cs] = jnp.ones_like(mb)
        over_s[:, cs] = jnp.zeros_like(mb)

    def probs(j, slot, bias, cs):
        rows = bk if bias is None else (bias + 1) * STRIP
        s = scores(j, False, cs, rows=rows)
        if bias is not None:
            s = s + bias_s[bias, 0:rows, :]
        r_old = r_s[:, cs]
        pbuf[slot][0:rows, cs] = jnp.exp2(s - r_old).astype(BF16)
        mb = jnp.max(s, axis=0, keepdims=True)
        r_new = jnp.maximum(r_old, mb)
        abuf[1 - slot][:, cs] = jnp.exp2(r_old - r_new)
        over_s[:, cs] = jnp.maximum(over_s[:, cs], mb - r_old)
        r_s[:, cs] = r_new

    def values(j, slot, cs, rows=bk):
        off = pl.multiple_of(j * bk, bk)
        upd = jnp.dot(vt_s[:, pl.ds(off, rows)], pbuf[slot][0:rows, cs],
                      preferred_element_type=F32)
        acc_s[:, cs] = acc_s[:, cs] * abuf[slot][:, cs] + upd

    def pair(j):
        for cs in strips:
            values(j, 0, cs)
            probs(j + 1, 1, None, cs)
        for cs in strips:
            values(j + 1, 1, cs)
            probs(j + 2, 0, None, cs)

    def pairs(first, count):
        for i in range(count):
            pair(2 * (first + i))

    def main_body(u, carry):
        pairs(unroll * u, unroll)
        return carry

    def visibility(d, cs):
        lo = cs.start % bq
        if lo + STRIP <= d * bk:
            return "none"
        return "all" if lo >= (d + 1) * bk else "part"

    def diag_bias(d, cs):
        return (cs.start % bq - d * bk) // STRIP if visibility(d, cs) == "part" else None

    def diag_probs(d, cs):
        if visibility(d, cs) != "none":
            probs(first_diag + d, d % 2, diag_bias(d, cs), cs)

    def diag_values(d, cs):
        if visibility(d, cs) != "none":
            t = diag_bias(d, cs)
            values(first_diag + d, d % 2, cs, rows=bk if t is None else (t + 1) * STRIP)

    def first_block(diagonal):
        for group in groups:
            load_queries(group)
            for cs in group:
                first_reference(cs)
                probs(0, 0, diag_bias(0, cs) if diagonal else None, cs)

    @pl.when(qi == 0)
    def _():
        first_block(True)

    @pl.when(qi >= 1)
    def _():
        first_block(False)

    n_pairs = jnp.maximum(first_diag // 2 - 1, 0)
    n_main = n_pairs // unroll
    lax.fori_loop(0, n_main, main_body, 0)
    left = n_pairs - n_main * unroll

    if unroll == 4:
        @pl.when(left >= 2)
        def _():
            pairs(n_main * unroll, 2)

    odd_pair = jnp.bitwise_and(n_pairs, 1) == 1

    def diagonal_blocks():
        for d in range(1, ratio):
            for cs in strips:
                diag_values(d - 1, cs)
                diag_probs(d, cs)
        for cs in strips:
            diag_values(ratio - 1, cs)
        for group in groups:
            store_output(group)

    def last_blocks(with_pair):
        if with_pair:
            pair(2 * (n_pairs - 1))
        for cs in strips:
            values(first_diag - 2, 0, cs)
            probs(first_diag - 1, 1, None, cs)
        for cs in strips:
            values(first_diag - 1, 1, cs)
            diag_probs(0, cs)
        diagonal_blocks()

    @pl.when(jnp.logical_and(qi >= 1, odd_pair))
    def _():
        last_blocks(True)

    if ratio % 4 != 0:
        @pl.when(jnp.logical_and(qi >= 1, jnp.logical_not(odd_pair)))
        def _():
            last_blocks(False)

    @pl.when(qi == 0)
    def _():
        diagonal_blocks()

    @pl.when(jnp.max(over_s[...]) > gap_limit)
    def _():
        r_s[...] = jnp.full(r_s.shape, -jnp.inf, F32)
        acc_s[...] = jnp.zeros(acc_s.shape, F32)

        def exact(i, carry):
            j = i // len(strips)
            lane0 = pl.multiple_of((i % len(strips)) * STRIP, STRIP)
            cs = pl.ds(lane0, STRIP)
            off = pl.multiple_of(j * bk, bk)
            s = jnp.dot(k_ref[pl.ds(off, bk), :], qt_s[:, cs], preferred_element_type=F32)
            key = lax.broadcasted_iota(jnp.int32, s.shape, 0) + (j * bk - qi * bq)
            qry = jnp.bitwise_and(lax.broadcasted_iota(jnp.int32, s.shape, 1) + lane0, bq - 1)
            s = jnp.where(key <= qry, s, -jnp.inf)
            m_old = r_s[:, cs]
            m_new = jnp.maximum(m_old, jnp.max(s, axis=0, keepdims=True))
            p = jnp.exp2(s - m_new).astype(BF16)
            upd = jnp.dot(vt_s[:, pl.ds(off, bk)], p, preferred_element_type=F32)
            acc_s[:, cs] = acc_s[:, cs] * jnp.exp2(m_old - m_new) + upd
            r_s[:, cs] = m_new
            return carry

        lax.fori_loop(0, n_blocks * len(strips), exact, 0)
        for group in groups:
            store_output(group)


def _attention(q, k, v, extras, *, batch, seq, heads, qk_dim, q_col, k_col, v_col, bq, bk, unroll, diff,
               name, gap_limit=GAP_LIMIT):
    n_maps = 2 if diff else 1
    nq = n_maps * bq
    nqb = seq // bq
    assert bq % (2 * bk) == 0 and bk % STRIP == 0 and seq % bq == 0 and unroll in (2, 4)
    kernel = functools.partial(_attn_kernel, n_maps=n_maps, bq=bq, bk=bk, diff=diff, gap_limit=gap_limit,
                               unroll=unroll)
    in_specs = [
        pl.BlockSpec((bq, qk_dim), lambda b, h, i: (b * nqb + i, q_col + h)),
        pl.BlockSpec((seq, qk_dim), lambda b, h, i: (b, k_col + h)),
        pl.BlockSpec((seq, LANES), lambda b, h, i: (b, v_col + h)),
    ]
    in_specs += [spec for _, spec in extras]
    row = pltpu.VMEM((1, nq), F32)
    return pl.pallas_call(
        kernel,
        grid=(batch, heads, nqb),
        in_specs=in_specs,
        out_specs=pl.BlockSpec((bq, LANES), lambda b, h, i: (b * nqb + i, h)),
        out_shape=jax.ShapeDtypeStruct((batch * seq, heads * LANES), BF16),
        scratch_shapes=[
            pltpu.VMEM((qk_dim, nq), BF16),
            pltpu.VMEM((LANES + ONES_ROWS, seq), BF16),
            pltpu.VMEM((bk // STRIP, bk, STRIP), F32),
            pltpu.VMEM((bk, nq), BF16), pltpu.VMEM((bk, nq), BF16),
            row, row, row, row,
            pltpu.VMEM((LANES + ONES_ROWS, nq), F32),
        ],
        compiler_params=_cparams(("parallel", "parallel", "arbitrary")),
        name=name,
    )(q, k, v, *[a for a, _ in extras])


def _gla_kernel(gq_ref, gk_ref, gv_ref, go_ref, gr_ref, gw_ref, gb_ref, gn_ref, o_ref, st_ref, *,
                block, chunk):
    c = chunk

    @pl.when(pl.program_id(1) == 0)
    def _():
        st_ref[...] = jnp.zeros(st_ref.shape, F32)

    logits = jnp.dot(gr_ref[...], gw_ref[...], preferred_element_type=F32) + gb_ref[...]
    soft = jnp.log2(1.0 + jnp.exp2(jnp.abs(logits) * (-LOG2E)))
    la = (jnp.minimum(logits, 0.0) * LOG2E - soft) * (1.0 / GLA_TAU)

    row = lax.broadcasted_iota(jnp.int32, la.shape, 0)
    in_chunk = jnp.bitwise_and(row, c - 1)
    b_all = la
    s = 1
    while s < c:
        b_all = b_all + jnp.where(in_chunk >= s, pltpu.roll(b_all, s, 0), 0.0)
        s *= 2

    lane = lax.broadcasted_iota(jnp.int32, (c, LANES), 1)
    first_head = lane < GLA_DK
    crow = lax.broadcasted_iota(jnp.int32, (c, GLA_HEADS * GLA_DK), 0)
    ri = lax.broadcasted_iota(jnp.int32, (c, c), 0)
    ci = lax.broadcasted_iota(jnp.int32, (c, c), 1)
    lvl = jnp.where(ri > ci, jnp.bitwise_xor(ri, ci), 0)
    diag = ri == ci
    n_levels = c.bit_length() - 1
    level_masks = [lax.shift_right_logical(lvl, bit) == 1 for bit in range(n_levels)]
    er = lax.broadcasted_iota(jnp.int32, (2 * GLA_DV, LANES), 0)
    ec = lax.broadcasted_iota(jnp.int32, (2 * GLA_DV, LANES), 1)
    bd_mask = (er < GLA_DV) == (ec < GLA_DK)
    gn = gn_ref[...]

    def head_scores(qt, kt):
        out = []
        for p in range(GLA_HEADS // 2):
            qp = qt[:, p * LANES:(p + 1) * LANES]
            kp = kt[:, p * LANES:(p + 1) * LANES].astype(BF16)
            for hh in range(2):
                qh = jnp.where(first_head if hh == 0 else jnp.logical_not(first_head), qp, 0.0)
                out.append(lax.dot_general(qh.astype(BF16), kp, (((1,), (1,)), ((), ())),
                                           preferred_element_type=F32))
        return out

    for sc in range(block // c):
        rows = slice(sc * c, (sc + 1) * c)
        b = b_all[rows]
        q = gq_ref[rows, :].astype(F32) * (GLA_DK ** -0.5)
        k = gk_ref[rows, :].astype(F32)
        v = gv_ref[rows, :]

        attn = [jnp.where(diag, m, 0.0) for m in head_scores(q, k)]
        ref_b = b
        for bit in range(n_levels):
            hs = 1 << bit
            if hs > 1:
                ref_b = jnp.where(jnp.bitwise_and(crow, hs - 1) < hs // 2, ref_b,
                                  pltpu.roll(ref_b, hs // 2, 0))
            qt = q * jnp.exp2(jnp.minimum(b - ref_b, 0.0))
            nxt = pltpu.roll(ref_b, c - hs, 0)
            kt = k * jnp.exp2(jnp.minimum(nxt - b, 0.0))
            attn = [jnp.where(level_masks[bit], m, a) for m, a in zip(head_scores(qt, kt), attn)]

        b_end = b[c - 1:c, :]
        qb = (q * jnp.exp2(b)).astype(BF16)
        kd = (k * jnp.exp2(b_end - b)).astype(BF16)

        for p in range(GLA_HEADS // 2):
            st = st_ref[p]
            inter = lax.dot_general(qb[:, p * LANES:(p + 1) * LANES], st.astype(BF16),
                                    (((1,), (1,)), ((), ())), preferred_element_type=F32)
            vp = v[:, p * 2 * GLA_DV:(p + 1) * 2 * GLA_DV]
            for hh in range(2):
                h = 2 * p + hh
                cols = slice(h * GLA_DV, (h + 1) * GLA_DV)
                intra = jnp.dot(attn[h].astype(BF16), vp[:, hh * GLA_DV:(hh + 1) * GLA_DV],
                                preferred_element_type=F32)
                out = inter[:, hh * GLA_DV:(hh + 1) * GLA_DV] + intra
                g = go_ref[rows, cols].astype(F32)
                o_ref[rows, cols] = (_rms(out, gn) * (g / (1.0 + jnp.exp(-g)))).astype(o_ref.dtype)
            upd = lax.dot_general(vp, kd[:, p * LANES:(p + 1) * LANES], (((0,), (0,)), ((), ())),
                                  preferred_element_type=F32)
            st_ref[p] = (st * jnp.exp2(b_end[:, p * LANES:(p + 1) * LANES])
                         + jnp.where(bd_mask, upd, 0.0))


def _gla(proj, gw, gb, gn, layer, *, batch, seq, block, chunk):
    nc = seq // block
    qk = GLA_HEADS * GLA_DK
    vd = GLA_HEADS * GLA_DV
    assert block % chunk == 0 and chunk & (chunk - 1) == 0
    return pl.pallas_call(
        functools.partial(_gla_kernel, block=block, chunk=chunk),
        grid=(batch, nc),
        in_specs=[
            pl.BlockSpec((block, qk), lambda b, c: (b * nc + c, COL_GQ // qk)),
            pl.BlockSpec((block, qk), lambda b, c: (b * nc + c, COL_GK // qk)),
            pl.BlockSpec((block, vd), lambda b, c: (b * nc + c, COL_GV // vd)),
            pl.BlockSpec((block, vd), lambda b, c: (b * nc + c, COL_GO // vd)),
            pl.BlockSpec((block, LANES), lambda b, c: (b * nc + c, COL_MKR // LANES)),
            pl.BlockSpec((None, LANES, qk), lambda b, c: (layer, 0, 0)),
            pl.BlockSpec((None, 1, qk), lambda b, c: (layer, 0, 0)),
            pl.BlockSpec((None, 1, GLA_DV), lambda b, c: (layer, 0, 0)),
        ],
        out_specs=pl.BlockSpec((block, vd), lambda b, c: (b * nc + c, 0)),
        out_shape=jax.ShapeDtypeStruct((batch * seq, vd), BF16),
        scratch_shapes=[pltpu.VMEM((GLA_HEADS // 2, 2 * GLA_DV, LANES), F32)],
        compiler_params=_cparams(("parallel", "arbitrary")),
        name="gla",
    )(proj, proj, proj, proj, proj, gw, gb, gn)


def _out_proj_kernel(x_ref, a_ref, b_ref, c_ref, w_ref, o_ref):
    na, nb = a_ref.shape[1], b_ref.shape[1]
    acc = jnp.dot(a_ref[...], w_ref[0:na, :], preferred_element_type=F32)
    acc += jnp.dot(b_ref[...], w_ref[na:na + nb, :], preferred_element_type=F32)
    acc += jnp.dot(c_ref[...], w_ref[na + nb:, :], preferred_element_type=F32)
    o_ref[...] = x_ref[...] + acc


def _out_proj(x, oa, ob, oc, w, layer, *, tm):
    t, d = x.shape

    def act(a):
        return pl.BlockSpec((tm, a.shape[1]), lambda i: (i, 0))

    return pl.pallas_call(
        _out_proj_kernel,
        grid=(t // tm,),
        in_specs=[act(x), act(oa), act(ob), act(oc),
                  pl.BlockSpec((None, w.shape[1], d), lambda i: (layer, 0, 0),
                               pipeline_mode=pl.Buffered(1))],
        out_specs=pl.BlockSpec((tm, d), lambda i: (i, 0)),
        out_shape=jax.ShapeDtypeStruct((t, d), F32),
        compiler_params=_cparams(("parallel",)),
        name="out_proj",
    )(x, oa, ob, oc, w)


def _ffn_kernel(x_ref, g_ref, wg_ref, wu_ref, wd_ref, fg_ref, o_ref, h_ref, *, final_norm):
    f = pl.program_id(1)

    def swiglu(h):
        gate = jnp.dot(h, wg_ref[...], preferred_element_type=F32)
        up = jnp.dot(h, wu_ref[...], preferred_element_type=F32)
        act = (gate / (1.0 + jnp.exp(-gate)) * up).astype(BF16)
        return jnp.dot(act, wd_ref[...], preferred_element_type=F32)

    @pl.when(f == 0)
    def _():
        for s in range(x_ref.shape[0] // FFN_SLAB):
            rows = slice(s * FFN_SLAB, (s + 1) * FFN_SLAB)
            x = x_ref[rows, :]
            h = _rms(x, g_ref[...]).astype(BF16)
            h_ref[rows, :] = h
            o_ref[rows, :] = x + swiglu(h)

    @pl.when(f > 0)
    def _():
        o_ref[...] += swiglu(h_ref[...])

    if final_norm:
        @pl.when(f == pl.num_programs(1) - 1)
        def _():
            o_ref[...] = _rms(o_ref[...], fg_ref[...])


def _ffn(x, g, wg, wu, wd, fg, layer, *, tm, tf, final_norm):
    t, d = x.shape
    ff = wg.shape[-1]
    return pl.pallas_call(
        functools.partial(_ffn_kernel, final_norm=final_norm),
        grid=(t // tm, ff // tf),
        in_specs=[
            pl.BlockSpec((tm, d), lambda i, f: (i, 0)),
            pl.BlockSpec((None, 1, d), lambda i, f: (layer, 0, 0)),
            pl.BlockSpec((None, d, tf), lambda i, f: (layer, 0, f)),
            pl.BlockSpec((None, d, tf), lambda i, f: (layer, 0, f)),
            pl.BlockSpec((None, tf, d), lambda i, f: (layer, f, 0)),
            pl.BlockSpec((1, d), lambda i, f: (0, 0)),
        ],
        out_specs=pl.BlockSpec((tm, d), lambda i, f: (i, 0)),
        out_shape=jax.ShapeDtypeStruct((t, d), F32),
        scratch_shapes=[pltpu.VMEM((tm, d), BF16)],
        compiler_params=_cparams(("parallel", "arbitrary")),
        name="ffn",
    )(x, g, wg, wu, wd, fg)


def _prep_w_in(w_in):
    sizes = (768, 768, 768, 256, 256, 512, 16, 512, 512, 256, 64)
    offs = [0]
    for s in sizes:
        offs.append(offs[-1] + s)
    d_q, d_k, d_v, g_q, g_k, g_v, g_r, g_o, m_qa, m_kva, m_kr = (
        w_in[..., offs[i]:offs[i + 1]] for i in range(len(sizes)))
    used = COL_GR + GLA_GATE_RANK
    pad = jnp.zeros(w_in.shape[:-1] + (PROJ_COLS - used,), w_in.dtype)
    parts = [d_q, d_k, d_v, g_q, g_k, m_kva, g_v, g_o, m_qa, m_kr, g_r, pad]
    return jnp.concatenate([p.astype(BF16) for p in parts], axis=-1)


def _prep_w_q_b(w):
    depth, r, _ = w.shape
    w = w.reshape(depth, r, MLA_HEADS, MLA_NOPE + MLA_ROPE)
    w = jnp.pad(w, ((0, 0), (0, 0), (0, 0), (0, MLA_QK - MLA_NOPE - MLA_ROPE)))
    return w.reshape(depth, r, MLA_HEADS * MLA_QK).astype(BF16)


def _prep_w_kv_b(w):
    depth, r, _ = w.shape
    w = w.reshape(depth, r, MLA_HEADS, MLA_NOPE + MLA_VD)
    k = w[..., :MLA_NOPE].reshape(depth, r, MLA_HEADS * MLA_NOPE)
    v = w[..., MLA_NOPE:].reshape(depth, r, MLA_HEADS * MLA_VD)
    return jnp.concatenate([k, v], axis=-1).astype(BF16)


def _prep_gate_w(w):
    depth = w.shape[0]
    lo = COL_GR - COL_MKR
    out = jnp.zeros((depth, LANES, w.shape[-1]), w.dtype)
    return out.at[:, lo:lo + GLA_GATE_RANK, :].set(w).astype(BF16)


def kernel(x, positions, attn_norm, w_in, diff_lambda_q1, diff_lambda_k1, diff_lambda_q2, diff_lambda_k2, diff_subln, gla_gate_w, gla_gate_b, gla_norm, mla_q_norm, mla_w_q_b, mla_kv_norm, mla_w_kv_b, w_out, ffn_norm, w_gate, w_up, w_down, final_norm):
    batch, seq, d = x.shape
    depth = w_in.shape[0]
    t = batch * seq
    assert d == D_MODEL and seq % max(TILES["diff_q"], TILES["mla_q"]) == 0

    w_in_p = _prep_w_in(w_in)
    wq_p = _prep_w_q_b(mla_w_q_b)
    wkv_p = _prep_w_kv_b(mla_w_kv_b)
    gw_p = _prep_gate_w(gla_gate_w)
    wo_p = w_out.astype(BF16)
    wg_p, wu_p, wd_p = w_gate.astype(BF16), w_up.astype(BF16), w_down.astype(BF16)
    lam_p = jnp.stack([diff_lambda_q1, diff_lambda_k1, diff_lambda_q2, diff_lambda_k2], axis=1).astype(F32)
    lam_init = jnp.asarray([0.8 - 0.6 * math.exp(-0.3 * l) for l in range(depth)], F32)
    cst = jnp.zeros((depth, 1, LANES), F32).at[:, 0, 0].set(lam_init)

    def row3(a):
        return a.astype(F32)[:, None, :]

    posb = jnp.broadcast_to(positions.astype(F32).reshape(t, 1), (t, LANES))
    half = MLA_ROPE // 2
    inv = 1.0 / (ROPE_THETA ** (jnp.arange(0, MLA_ROPE, 2, dtype=F32) / MLA_ROPE))
    inv = jnp.concatenate([inv, inv, jnp.zeros((LANES - 2 * half,), F32)])[None, :]
    cos, sa, sb = _rope_tables(posb, inv, tm=TILES["rope_rows"])

    xs = x.reshape(t, d)
    for l in range(depth):
        proj = _norm_matmul(xs, row3(attn_norm), w_in_p, l, tm=TILES["in_proj_rows"])
        mq, mk, mv = _mla_prep(proj, row3(mla_q_norm), row3(mla_kv_norm), wq_p, wkv_p, cos, sa, sb, l,
                               tm=TILES["mla_prep_rows"])
        diff_extras = [
            (lam_p, pl.BlockSpec((None, 4, DIFF_HD), lambda b, h, i, l=l: (l, 0, 0))),
            (cst, pl.BlockSpec((None, 1, LANES), lambda b, h, i, l=l: (l, 0, 0))),
            (row3(diff_subln), pl.BlockSpec((None, 1, DIFF_VD), lambda b, h, i, l=l: (l, 0, 0))),
        ]
        o_diff = _attention(proj, proj, proj, diff_extras, batch=batch, seq=seq, heads=DIFF_HEADS,
                            qk_dim=LANES, q_col=COL_DQ // LANES, k_col=COL_DK // LANES,
                            v_col=COL_DV // LANES, bq=TILES["diff_q"], bk=TILES["diff_k"],
                            unroll=TILES["diff_unroll"], diff=True, name="diff_attn")
        o_gla = _gla(proj, gw_p, row3(gla_gate_b), row3(gla_norm), l, batch=batch, seq=seq,
                     block=TILES["gla_block"], chunk=TILES["gla_chunk"])
        o_mla = _attention(mq, mk, mv, [], batch=batch, seq=seq, heads=MLA_HEADS,
                           qk_dim=MLA_QK, q_col=0, k_col=0, v_col=0, bq=TILES["mla_q"],
                           bk=TILES["mla_k"], unroll=TILES["mla_unroll"], diff=False, name="mla_attn")
        x1 = _out_proj(xs, o_diff, o_gla, o_mla, wo_p, l, tm=TILES["out_proj_rows"])
        xs = _ffn(x1, row3(ffn_norm), wg_p, wu_p, wd_p, final_norm.astype(F32)[None, :], l,
                  tm=TILES["ffn_rows"], tf=TILES["ffn_cols"], final_norm=(l == depth - 1))
    return xs.reshape(batch, seq, d)
```
